```python
import math
import jax
import jax.numpy as jnp
from jax import lax
import numpy as np

D_MODEL = 1024
BATCH = 4
SEQ = 4096
DEPTH = 4

GRID_W = 64
CTX_LEN = 256
N_BRANCH = 4
BRANCH_W = 512
Q_BLOCK = 128
ROPE_BASE = 10000.0
EPS = 1e-6
DA_HEADS = 4
DA_HEAD_DIM = 64
SSD_HEADS = 8
SSD_HEAD_DIM = 64
SSD_GROUPS = 2
SSD_HPG = SSD_HEADS // SSD_GROUPS
SSD_STATE = 128
SSD_CONV = 5
SSD_CHUNK = 128
GQA_HEADS = 8
GQA_KV_HEADS = 2
GQA_HPG = GQA_HEADS // GQA_KV_HEADS
GQA_HEAD_DIM = 64
ROPE_DIM = 64
CONF_CONV = 31
D_FF = 2816
FFN_CONV = 3

DA_Q = 2 * DA_HEADS * DA_HEAD_DIM
DA_K = 2 * DA_HEADS * DA_HEAD_DIM
DA_V = DA_HEADS * 2 * DA_HEAD_DIM
SSD_Z = SSD_HEADS * SSD_HEAD_DIM
SSD_BC = SSD_GROUPS * SSD_STATE
SSD_XBC = SSD_Z + 2 * SSD_BC
SSD_DT = 2 * SSD_HEADS
GQA_Q = GQA_HEADS * GQA_HEAD_DIM
GQA_K = GQA_KV_HEADS * GQA_HEAD_DIM
GQA_V = GQA_KV_HEADS * GQA_HEAD_DIM
CONF_IN = 2 * BRANCH_W
GATE_COLS = N_BRANCH * D_MODEL
IN_SPLITS = (DA_Q, DA_K, DA_V, SSD_Z, SSD_XBC, SSD_DT, GQA_Q, GQA_K, GQA_V, CONF_IN, GATE_COLS)
IN_COLS = sum(IN_SPLITS)
IN_OFFSETS = [int(v) for v in np.cumsum(IN_SPLITS)[:-1]]

kernel_name = "hybrid_gated_dit_trunk"

F32 = jnp.float32


def _rmsnorm(x, g):
    xf = x.astype(F32)
    y = xf * lax.rsqrt(jnp.mean(xf * xf, axis=-1, keepdims=True) + EPS)
    return (y * g.astype(F32)).astype(x.dtype)


def _layernorm(x, g, b):
    xf = x.astype(F32)
    xc = xf - jnp.mean(xf, axis=-1, keepdims=True)
    y = xc * lax.rsqrt(jnp.mean(xc * xc, axis=-1, keepdims=True) + EPS)
    return (y * g.astype(F32) + b.astype(F32)).astype(x.dtype)


def _modulate(x, g, shift, scale):
    return _rmsnorm(x, g) * (1.0 + scale) + shift


def _dwconv(x, w, b):
    pad = w.shape[0] // 2
    y = lax.conv_general_dilated(x, w[:, None, :].astype(x.dtype), window_strides=(1,),
                                 padding=[(pad, pad)], dimension_numbers=('NWC', 'WIO', 'NWC'),
                                 feature_group_count=x.shape[-1])
    return y + b.astype(x.dtype)


def _axial_rope_tables(rows, head_dim):
    row = jnp.repeat(jnp.arange(rows, dtype=F32), GRID_W)
    col = jnp.tile(jnp.arange(GRID_W, dtype=F32), rows)
    quarter = head_dim // 4
    inv_freq = ROPE_BASE ** (-jnp.arange(quarter, dtype=F32) / quarter)
    ang = jnp.concatenate([row[:, None] * inv_freq, col[:, None] * inv_freq], axis=-1)
    return jnp.cos(ang), jnp.sin(ang)


def _apply_rope(x, cos, sin):
    half = x.shape[-1] // 2
    shape = (1, x.shape[1]) + (1,) * (x.ndim - 3) + (half,)
    cs, sn = cos.reshape(shape), sin.reshape(shape)
    xf = x.astype(F32)
    x1, x2 = xf[..., :half], xf[..., half:]
    return jnp.concatenate([x1 * cs - x2 * sn, x1 * sn + x2 * cs], axis=-1).astype(x.dtype)


def _sweep_query_blocks(fn, q):
    b, n = q.shape[0], q.shape[1]
    nb = n // Q_BLOCK
    qb = jnp.moveaxis(q.reshape((b, nb, Q_BLOCK) + q.shape[2:]), 1, 0)
    out = lax.map(fn, qb)
    return jnp.moveaxis(out, 0, 1).reshape((b, n) + out.shape[3:])


def _diff_attend(q, k, v, lam):
    s = jnp.einsum('bqthd,bkthd->bthqk', q, k).astype(F32) * (DA_HEAD_DIM ** -0.5)
    p = jax.nn.softmax(s, axis=-1)
    a = p[:, 0] - lam * p[:, 1]
    return jnp.einsum('bhqk,bkhe->bqhe', a.astype(v.dtype), v)


def _gqa_attend(q, k, v):
    s = jnp.einsum('bqgrd,bkgd->bgrqk', q, k).astype(F32) * (GQA_HEAD_DIM ** -0.5)
    p = jax.nn.softmax(s, axis=-1)
    return jnp.einsum('bgrqk,bkgd->bqgrd', p.astype(v.dtype), v)


def _ssd_scan(x, dt, a, bm, cm, h0, need_y):
    b, n, g, r, p = x.shape
    nc = n // SSD_CHUNK
    xs = (x.astype(F32) * dt[..., None]).reshape(b, nc, SSD_CHUNK, g, r, p)
    da = (dt * a).reshape(b, nc, SSD_CHUNK, g, r)
    bc = bm.astype(F32).reshape(b, nc, SSD_CHUNK, g, SSD_STATE)
    cc = cm.astype(F32).reshape(b, nc, SSD_CHUNK, g, SSD_STATE)
    cs = jnp.cumsum(da, axis=2)
    decay_to_end = jnp.exp(cs[:, :, -1:] - cs)
    states = jnp.einsum('bcjgs,bcjgr,bcjgrp->bcgrps', bc, decay_to_end, xs)
    chunk_decay = jnp.exp(cs[:, :, -1])

    def step(h, inp):
        st, dec = inp
        return h * dec[..., None, None] + st, h

    h_last, h_prev = lax.scan(step, h0, (jnp.moveaxis(states, 1, 0), jnp.moveaxis(chunk_decay, 1, 0)))
    if not need_y:
        return None, h_last
    h_prev = jnp.moveaxis(h_prev, 0, 1)
    lower = jnp.tril(jnp.ones((SSD_CHUNK, SSD_CHUNK), dtype=bool))
    seg = cs[:, :, :, None] - cs[:, :, None, :]
    lmat = jnp.exp(jnp.where(lower[None, None, :, :, None, None], seg, -jnp.inf))
    cb = jnp.einsum('bcigs,bcjgs->bcijg', cc, bc)
    y_diag = jnp.einsum('bcijgr,bcjgrp->bcigrp', cb[..., None] * lmat, xs)
    y_off = jnp.einsum('bcigs,bcgrps->bcigrp', cc, h_prev) * jnp.exp(cs)[..., None]
    y = (y_diag + y_off).reshape(b, n, g, r, p)
    return y.astype(x.dtype), h_last


def _ssd_bidirectional(inp_c, inp_l, a_log, dt_bias, d_skip, need_ctx):
    xc, bc, cc, dtc = inp_c
    xl, bl, cl, dtl = inp_l
    h0 = jnp.zeros((xl.shape[0], SSD_GROUPS, SSD_HPG, SSD_HEAD_DIM, SSD_STATE), F32)
    y_c, y_l = [], []
    for d in range(2):
        flip = (lambda t: jnp.flip(t, axis=1)) if d == 1 else (lambda t: t)
        a = -jnp.exp(a_log[d].astype(F32)).reshape(SSD_GROUPS, SSD_HPG)
        bias = dt_bias[d].astype(F32).reshape(SSD_GROUPS, SSD_HPG)
        dt_c = jax.nn.softplus(dtc[:, :, d].astype(F32) + bias)
        dt_l = jax.nn.softplus(dtl[:, :, d].astype(F32) + bias)
        skip = d_skip[d].reshape(SSD_GROUPS, SSD_HPG, 1)
        yc, hc = _ssd_scan(flip(xc), flip(dt_c), a, flip(bc), flip(cc), h0, need_ctx)
        yl, _ = _ssd_scan(flip(xl), flip(dt_l), a, flip(bl), flip(cl), hc, True)
        y_l.append(flip(yl) + skip * xl)
        if need_ctx:
            y_c.append(flip(yc) + skip * xc)
    y_ctx = (y_c[0] + y_c[1]) if need_ctx else None
    return y_ctx, y_l[0] + y_l[1]


def _hybrid_mixer(h_c, h_l, cos, sin, need_ctx, layer_idx, w_in, da_lambda, da_subln,
                  ssd_conv_w, ssd_conv_b, ssd_a_log, ssd_dt_bias, ssd_d, ssd_norm,
                  gqa_q_norm, gqa_k_norm, conf_conv_w, conf_conv_b, conf_ln_g, conf_ln_b,
                  w_branch, w_out):
    b, nc, nl = h_l.shape[0], h_c.shape[1], h_l.shape[1]
    (da_q_c, da_k_c, da_v_c, z_c, xbc_c, dt_c, gq_c, gk_c, gv_c, conf_c, gate_c) = jnp.split(h_c @ w_in, IN_OFFSETS, axis=-1)
    (da_q_l, da_k_l, da_v_l, z_l, xbc_l, dt_l, gq_l, gk_l, gv_l, conf_l, gate_l) = jnp.split(h_l @ w_in, IN_OFFSETS, axis=-1)

    lambda_init = 0.8 - 0.6 * math.exp(-0.3 * layer_idx)
    lq1, lk1, lq2, lk2 = da_lambda.astype(F32)
    lam = jnp.exp(jnp.sum(lq1 * lk1)) - jnp.exp(jnp.sum(lq2 * lk2)) + lambda_init

    def da_qk(t, n):
        return t.reshape(b, n, 2, DA_HEADS, DA_HEAD_DIM)

    def da_v(t, n):
        return t.reshape(b, n, DA_HEADS, 2 * DA_HEAD_DIM)

    def da_out(o, n):
        return (_rmsnorm(o, da_subln) * (1.0 - lambda_init)).reshape(b, n, BRANCH_W)

    ka_c, va_c = da_qk(da_k_c, nc), da_v(da_v_c, nc)
    qa_l = _apply_rope(da_qk(da_q_l, nl), cos, sin)
    ka_all = jnp.concatenate([_apply_rope(da_qk(da_k_l, nl), cos, sin), ka_c], axis=1)
    va_all = jnp.concatenate([da_v(da_v_l, nl), va_c], axis=1)
    oa_l = _sweep_query_blocks(lambda qb: _diff_attend(qb, ka_all, va_all, lam), qa_l)

    def ssd_inputs(xbc, dt, n):
        xbc = jax.nn.silu(_dwconv(xbc, ssd_conv_w, ssd_conv_b))
        xs, bm, cm = jnp.split(xbc, [SSD_Z, SSD_Z + SSD_BC], axis=-1)
        return (xs.reshape(b, n, SSD_GROUPS, SSD_HPG, SSD_HEAD_DIM),
                bm.reshape(b, n, SSD_GROUPS, SSD_STATE),
                cm.reshape(b, n, SSD_GROUPS, SSD_STATE),
                dt.reshape(b, n, 2, SSD_GROUPS, SSD_HPG))

    def ssd_out(y, z, n):
        return _rmsnorm(y.reshape(b, n, BRANCH_W) * jax.nn.silu(z), ssd_norm)

    ys_c, ys_l = _ssd_bidirectional(ssd_inputs(xbc_c, dt_c, nc), ssd_inputs(xbc_l, dt_l, nl),
                                    ssd_a_log, ssd_dt_bias, ssd_d, need_ctx)

    def gqa_q(t, n):
        return _rmsnorm(t.reshape(b, n, GQA_HEADS, GQA_HEAD_DIM), gqa_q_norm)

    def gqa_k(t, n):
        return _rmsnorm(t.reshape(b, n, GQA_KV_HEADS, GQA_HEAD_DIM), gqa_k_norm)

    def gqa_v(t, n):
        return t.reshape(b, n, GQA_KV_HEADS, GQA_HEAD_DIM)

    kg_c, vg_c = gqa_k(gk_c, nc), gqa_v(gv_c, nc)
    qg_l = _apply_rope(gqa_q(gq_l, nl), cos, sin).reshape(b, nl, GQA_KV_HEADS, GQA_HPG, GQA_HEAD_DIM)
    kg_all = jnp.concatenate([_apply_rope(gqa_k(gk_l, nl), cos, sin), kg_c], axis=1)
    vg_all = jnp.concatenate([gqa_v(gv_l, nl), vg_c], axis=1)
    og_l = _sweep_query_blocks(lambda qb: _gqa_attend(qb, kg_all, vg_all), qg_l).reshape(b, nl, BRANCH_W)

    def conformer(u):
        a, g = jnp.split(u, 2, axis=-1)
        y = _dwconv(a * jax.nn.sigmoid(g), conf_conv_w, conf_conv_b)
        return jax.nn.silu(_layernorm(y, conf_ln_g, conf_ln_b))

    def merge(branches, gate_logits, n):
        m = jnp.stack(branches, axis=2)
        proj = jnp.einsum('bnke,ked->bnkd', m, w_branch)
        gates = jax.nn.sigmoid(gate_logits.reshape(b, n, N_BRANCH, D_MODEL))
        return jnp.sum(gates * proj, axis=2) @ w_out

    mix_l = merge([da_out(oa_l, nl), ssd_out(ys_l, z_l, nl), og_l, conformer(conf_l)], gate_l, nl)
    if not need_ctx:
        return None, mix_l
    oa_c = _diff_attend(da_qk(da_q_c, nc), ka_c, va_c, lam)
    og_c = _gqa_attend(gqa_q(gq_c, nc).reshape(b, nc, GQA_KV_HEADS, GQA_HPG, GQA_HEAD_DIM), kg_c, vg_c).reshape(b, nc, BRANCH_W)
    mix_c = merge([da_out(oa_c, nc), ssd_out(ys_c, z_c, nc), og_c, conformer(conf_c)], gate_c, nc)
    return mix_c, mix_l


def _conv_ffn(h, w_up, conv_w, conv_b, w_down):
    u = _dwconv(h @ w_up, conv_w, conv_b)
    a, g = jnp.split(u, 2, axis=-1)
    return (jax.nn.silu(g) * a) @ w_down


def setup_inputs(seed: int = 0) -> dict:
    key = jax.random.key(seed)
    ks = iter(jax.random.split(key, 40))

    def nrm(shape, scale):
        return scale * jax.random.normal(next(ks), shape, jnp.float32)

    def gain(shape):
        return 1.0 + nrm(shape, 0.05)

    L = DEPTH
    x = nrm((BATCH, SEQ, D_MODEL), 1.0)
    c = nrm((BATCH, D_MODEL), 1.0)
    ctx = nrm((BATCH, CTX_LEN, D_MODEL), 1.0)
    c_ctx = nrm((D_MODEL,), 1.0)
    w_mod = nrm((L, D_MODEL, 6 * D_MODEL), 0.5 * D_MODEL ** -0.5)
    b_mod = nrm((L, 6 * D_MODEL), 0.02)
    mix_norm_pre = gain((L, D_MODEL))
    mix_norm_post = gain((L, D_MODEL))
    w_in = nrm((L, D_MODEL, IN_COLS), D_MODEL ** -0.5)
    da_lambda = nrm((L, 4, DA_HEAD_DIM), 0.1)
    da_subln = gain((L, 2 * DA_HEAD_DIM))
    ssd_conv_w = nrm((L, SSD_CONV, SSD_XBC), SSD_CONV ** -0.5)
    ssd_conv_b = nrm((L, SSD_XBC), 0.02)
    ssd_a_log = jnp.log(jax.random.uniform(next(ks), (L, 2, SSD_HEADS), jnp.float32, 1.0, 16.0))
    dt0 = jnp.exp(jax.random.uniform(next(ks), (L, 2, SSD_HEADS), jnp.float32, math.log(1e-3), math.log(1e-1)))
    ssd_dt_bias = dt0 + jnp.log(-jnp.expm1(-dt0))
    ssd_d = gain((L, 2, SSD_HEADS))
    ssd_norm = gain((L, BRANCH_W))
    gqa_q_norm = gain((L, GQA_HEAD_DIM))
    gqa_k_norm = gain((L, GQA_HEAD_DIM))
    conf_conv_w = nrm((L, CONF_CONV, BRANCH_W), CONF_CONV ** -0.5)
    conf_conv_b = nrm((L, BRANCH_W), 0.02)
    conf_ln_g = gain((L, BRANCH_W))
    conf_ln_b = nrm((L, BRANCH_W), 0.02)
    w_branch = nrm((L, N_BRANCH, BRANCH_W, D_MODEL), BRANCH_W ** -0.5)
    w_out = nrm((L, D_MODEL, D_MODEL), D_MODEL ** -0.5)
    ffn_norm_pre = gain((L, D_MODEL))
    ffn_norm_post = gain((L, D_MODEL))
    ffn_w_up = nrm((L, D_MODEL, 2 * D_FF), D_MODEL ** -0.5)
    ffn_conv_w = nrm((L, FFN_CONV, 2 * D_FF), FFN_CONV ** -0.5)
    ffn_conv_b = nrm((L, 2 * D_FF), 0.02)
    ffn_w_down = nrm((L, D_FF, D_MODEL), D_FF ** -0.5)
    return {
        'x': x, 'c': c, 'ctx': ctx, 'c_ctx': c_ctx,
        'w_mod': w_mod, 'b_mod': b_mod, 'mix_norm_pre': mix_norm_pre, 'mix_norm_post': mix_norm_post,
        'w_in': w_in, 'da_lambda': da_lambda, 'da_subln': da_subln,
        'ssd_conv_w': ssd_conv_w, 'ssd_conv_b': ssd_conv_b, 'ssd_a_log': ssd_a_log,
        'ssd_dt_bias': ssd_dt_bias, 'ssd_d': ssd_d, 'ssd_norm': ssd_norm,
        'gqa_q_norm': gqa_q_norm, 'gqa_k_norm': gqa_k_norm,
        'conf_conv_w': conf_conv_w, 'conf_conv_b': conf_conv_b, 'conf_ln_g': conf_ln_g, 'conf_ln_b': conf_ln_b,
        'w_branch': w_branch, 'w_out': w_out,
        'ffn_norm_pre': ffn_norm_pre, 'ffn_norm_post': ffn_norm_post,
        'ffn_w_up': ffn_w_up, 'ffn_conv_w': ffn_conv_w, 'ffn_conv_b': ffn_conv_b, 'ffn_w_down': ffn_w_down,
    }


def reference(x, c, ctx, c_ctx, w_mod, b_mod, mix_norm_pre, mix_norm_post, w_in, da_lambda, da_subln,
              ssd_conv_w, ssd_conv_b, ssd_a_log, ssd_dt_bias, ssd_d, ssd_norm, gqa_q_norm, gqa_k_norm,
              conf_conv_w, conf_conv_b, conf_ln_g, conf_ln_b, w_branch, w_out,
              ffn_norm_pre, ffn_norm_post, ffn_w_up, ffn_conv_w, ffn_conv_b, ffn_w_down):
    rows = x.shape[1] // GRID_W
    cos, sin = _axial_rope_tables(rows, ROPE_DIM)
    silu_c = jax.nn.silu(c)
    silu_cc = jax.nn.silu(c_ctx)
    xl, xc = x, ctx
    for l in range(DEPTH):
        need_ctx = l < DEPTH - 1
        mod_l = jnp.split((silu_c @ w_mod[l] + b_mod[l])[:, None, :], 6, axis=-1)
        mod_c = jnp.split((silu_cc @ w_mod[l] + b_mod[l])[None, None, :], 6, axis=-1)
        hl = _modulate(xl, mix_norm_pre[l], mod_l[0], mod_l[1])
        hc = _modulate(xc, mix_norm_pre[l], mod_c[0], mod_c[1])
        mix_c, mix_l = _hybrid_mixer(hc, hl, cos, sin, need_ctx, l, w_in[l], da_lambda[l], da_subln[l],
                                     ssd_conv_w[l], ssd_conv_b[l], ssd_a_log[l], ssd_dt_bias[l], ssd_d[l], ssd_norm[l],
                                     gqa_q_norm[l], gqa_k_norm[l], conf_conv_w[l], conf_conv_b[l],
                                     conf_ln_g[l], conf_ln_b[l], w_branch[l], w_out[l])
        xl = xl + mod_l[2] * _rmsnorm(mix_l, mix_norm_post[l])
        fl = _conv_ffn(_modulate(xl, ffn_norm_pre[l], mod_l[3], mod_l[4]), ffn_w_up[l], ffn_conv_w[l], ffn_conv_b[l], ffn_w_down[l])
        xl = xl + mod_l[5] * _rmsnorm(fl, ffn_norm_post[l])
        if need_ctx:
            xc = xc + mod_c[2] * _rmsnorm(mix_c, mix_norm_post[l])
            fc = _conv_ffn(_modulate(xc, ffn_norm_pre[l], mod_c[3], mod_c[4]), ffn_w_up[l], ffn_conv_w[l], ffn_conv_b[l], ffn_w_down[l])
            xc = xc + mod_c[5] * _rmsnorm(fc, ffn_norm_post[l])
    return xl
```

```python
import functools
import math

import jax
import jax.numpy as jnp
from jax import lax
from jax.experimental import pallas as pl
from jax.experimental.pallas import tpu as pltpu

F32 = jnp.float32
BF16 = jnp.bfloat16

GRID_W = 64
ROPE_BASE = 10000.0
EPS = 1e-6
N_BRANCH = 4
BRANCH_W = 512
DA_HEADS = 4
DA_HEAD_DIM = 64
SSD_HEADS = 8
SSD_HEAD_DIM = 64
SSD_GROUPS = 2
SSD_HPG = SSD_HEADS // SSD_GROUPS
SSD_STATE = 128
SSD_CONV = 5
GQA_HEADS = 8
GQA_KV_HEADS = 2
GQA_HPG = GQA_HEADS // GQA_KV_HEADS
GQA_HEAD_DIM = 64
CONF_CONV = 31
FFN_CONV = 3

LANES = 128
TM = 256
HALO_A = 16
HALO_F = 8
TQ_SUB = 128
VMEM_LIMIT = 56 * 1024 * 1024

C_DAQ, C_DAK, C_DAV, C_Z, C_XBC, C_GQ, C_GK, C_GV, C_CONF, C_DT, C_END = (
    0, 512, 1024, 1536, 2048, 3072, 3584, 3712, 3840, 4864, 4992)


def _params(n_axes):
    return pltpu.CompilerParams(dimension_semantics=("arbitrary",) * n_axes, vmem_limit_bytes=VMEM_LIMIT)


def _sigmoid(v):
    return 1.0 / (1.0 + jnp.exp(-v))


def _silu(v):
    return v * _sigmoid(v)


def _rms(v, gain):
    return v * lax.rsqrt(jnp.mean(v * v, axis=-1, keepdims=True) + EPS) * gain


def _dot(a, b):
    return jnp.dot(a, b, preferred_element_type=F32)


def _dot_nt(a, b):
    return lax.dot_general(a, b, (((1,), (1,)), ((), ())), preferred_element_type=F32)


def _split2(v):
    hi = v.astype(BF16)
    lo = (v - hi.astype(F32)).astype(BF16)
    return hi, lo


def _mod_kernel(c_ref, w_ref, b_ref, o_ref):
    s = _silu(c_ref[...])
    hi, lo = _split2(s)
    lhs = jnp.concatenate([hi, lo], axis=0)
    r = _dot(lhs, w_ref[...].astype(BF16))
    o_ref[...] = r[0:8] + r[8:16] + b_ref[...]


def _modulation(cvec, w_mod, b_mod):
    depth, d, n = w_mod.shape
    tn = 1536
    return pl.pallas_call(
        _mod_kernel,
        grid=(depth, n // tn),
        in_specs=[pl.BlockSpec((8, d), lambda l, j: (0, 0)),
                  pl.BlockSpec((None, d, tn), lambda l, j: (l, 0, j)),
                  pl.BlockSpec((None, 1, tn), lambda l, j: (l, 0, j))],
        out_specs=pl.BlockSpec((None, 8, tn), lambda l, j: (l, 0, j)),
        out_shape=jax.ShapeDtypeStruct((depth, 8, n), F32),
        compiler_params=_params(2),
        name="modulation",
    )(cvec, w_mod, b_mod.reshape(depth, 1, n))


def _rope128(tb, cos, sin, low32):
    up = pltpu.roll(tb, LANES - 32, axis=1)
    dn = pltpu.roll(tb, 32, axis=1)
    return tb * cos + jnp.where(low32, up, dn) * sin


def _inproj_kernel(tiles_per_batch,
                   xm, xp, xn, mod, gpre, w, cos_r, sin_r, qn, kn, scw, scb, ccw, ccb, lng, lnb,
                   qa_o, ka_o, va_o, z_o, xc_o, dt_o, gq_o, gk_o, gv_o, cf_o,
                   sbuf, cbuf):
    ti = pl.program_id(0) % tiles_per_batch
    prev_ok = ti >= 2
    next_ok = jnp.logical_and(ti >= 1, ti <= tiles_per_batch - 2)
    shift, scale, gain = mod[0:1, :], mod[1:2, :], gpre[...]

    def normmod(v):
        return _rms(v, gain) * (1.0 + scale) + shift

    hm = normmod(xm[...]).astype(BF16)
    hp = jnp.where(prev_ok, normmod(xp[...]), 0.0).astype(BF16)
    hn = jnp.where(next_ok, normmod(xn[...]), 0.0).astype(BF16)
    hext = jnp.concatenate([hp, hm, hn], axis=0)

    cos, sin = cos_r[...], sin_r[...]
    lane = lax.broadcasted_iota(jnp.int32, (TM, LANES), 1)
    low32 = (lane % 64) < 32
    gi = lax.broadcasted_iota(jnp.int32, (LANES, LANES), 0) // 64
    gj = lax.broadcasted_iota(jnp.int32, (LANES, LANES), 1) // 64
    gsum = (gi == gj).astype(BF16)

    def headnorm(tb, g):
        ss = _dot((tb * tb).astype(BF16), gsum)
        return tb * lax.rsqrt(ss * (1.0 / GQA_HEAD_DIM) + EPS) * g

    for j in range(4):
        c0 = C_DAQ + j * LANES
        q = _dot(hm, w[:, c0:c0 + LANES])
        qa_o[:, j * LANES:(j + 1) * LANES] = (_rope128(q, cos, sin, low32) * (DA_HEAD_DIM ** -0.5)).astype(BF16)
        c0 = C_DAK + j * LANES
        k = _dot(hm, w[:, c0:c0 + LANES])
        ka_o[:, j * LANES:(j + 1) * LANES] = _rope128(k, cos, sin, low32).astype(BF16)
    va_o[...] = _dot(hm, w[:, C_DAV:C_Z]).astype(BF16)
    z_o[...] = _dot(hm, w[:, C_Z:C_XBC])
    dt_o[...] = _dot(hm, w[:, C_DT:C_END])

    for j in range(4):
        c0 = C_GQ + j * LANES
        q = headnorm(_dot(hm, w[:, c0:c0 + LANES]), qn[...])
        gq_o[:, j * LANES:(j + 1) * LANES] = (_rope128(q, cos, sin, low32) * (GQA_HEAD_DIM ** -0.5)).astype(BF16)
    k = headnorm(_dot(hm, w[:, C_GK:C_GV]), kn[...])
    gk_o[...] = _rope128(k, cos, sin, low32).astype(BF16)
    gv_o[...] = _dot(hm, w[:, C_GV:C_CONF]).astype(BF16)

    sbuf[...] = _dot(hext, w[:, C_XBC:C_GQ])
    acc = scb[...] + scw[0:1, :] * sbuf[pl.ds(HALO_A - SSD_CONV // 2, TM), :]
    for k_tap in range(1, SSD_CONV):
        acc = acc + scw[k_tap:k_tap + 1, :] * sbuf[pl.ds(HALO_A - SSD_CONV // 2 + k_tap, TM), :]
    xc_o[...] = _silu(acc)

    a = _dot(hext, w[:, C_CONF:C_CONF + BRANCH_W])
    g = _dot(hext, w[:, C_CONF + BRANCH_W:C_DT])
    cbuf[...] = a * _sigmoid(g)
    acc = ccb[...] + ccw[0:1, :] * cbuf[pl.ds(HALO_A - CONF_CONV // 2, TM), :]
    for k_tap in range(1, CONF_CONV):
        acc = acc + ccw[k_tap:k_tap + 1, :] * cbuf[pl.ds(HALO_A - CONF_CONV // 2 + k_tap, TM), :]
    xc = acc - jnp.mean(acc, axis=-1, keepdims=True)
    y = xc * lax.rsqrt(jnp.mean(xc * xc, axis=-1, keepdims=True) + EPS) * lng[...] + lnb[...]
    cf_o[...] = _silu(y).astype(BF16)


def _inproj(xs, mod, layer, tiles_per_batch, n_batch, gpre, w_a, cos_t, sin_t, qn, kn, scw, scb, ccw, ccb, lng, lnb):
    rows, d = xs.shape
    n_tiles = rows // TM
    hb = TM // HALO_A
    n_hblk = rows // HALO_A
    tpb = tiles_per_batch

    def mod_row(t):
        return jnp.where(t % tpb == 0, n_batch, t // tpb)

    def lyr(shape):
        return pl.BlockSpec((None,) + shape, lambda t: (layer,) + (0,) * len(shape))

    def out(width, dtype):
        return pl.BlockSpec((TM, width), lambda t: (t, 0)), jax.ShapeDtypeStruct((rows, width), dtype)

    outs = [out(512, BF16), out(512, BF16), out(512, BF16), out(512, F32), out(1024, F32), out(LANES, F32),
            out(512, BF16), out(LANES, BF16), out(LANES, BF16), out(512, BF16)]
    return pl.pallas_call(
        functools.partial(_inproj_kernel, tpb),
        grid=(n_tiles,),
        in_specs=[pl.BlockSpec((TM, d), lambda t: (t, 0)),
                  pl.BlockSpec((HALO_A, d), lambda t: (jnp.maximum(t * hb - 1, 0), 0)),
                  pl.BlockSpec((HALO_A, d), lambda t: (jnp.minimum((t + 1) * hb, n_hblk - 1), 0)),
                  pl.BlockSpec((None, None, 6, d), lambda t: (layer, mod_row(t), 0, 0)),
                  lyr((1, d)),
                  lyr((d, C_END)),
                  pl.BlockSpec((TM, LANES), lambda t: (t % tpb, 0)),
                  pl.BlockSpec((TM, LANES), lambda t: (t % tpb, 0)),
                  lyr((1, LANES)), lyr((1, LANES)),
                  lyr((SSD_CONV, 1024)), lyr((1, 1024)),
                  lyr((CONF_CONV, BRANCH_W)), lyr((1, BRANCH_W)), lyr((1, BRANCH_W)), lyr((1, BRANCH_W))],
        out_specs=[o[0] for o in outs],
        out_shape=[o[1] for o in outs],
        scratch_shapes=[pltpu.VMEM((TM + 2 * HALO_A, 1024), F32), pltpu.VMEM((TM + 2 * HALO_A, BRANCH_W), F32)],
        compiler_params=_params(1),
        name="inproj",
    )(xs, xs, xs, mod, gpre, w_a, cos_t, sin_t, qn, kn, scw, scb, ccw, ccb, lng, lnb)


def _softmax_parts(s):
    m = jnp.max(s, axis=-1, keepdims=True)
    p = jnp.exp(s - m)
    return p, jnp.sum(p, axis=-1, keepdims=True)


def _da_kernel(lambda_init, q_ref, k_ref, v_ref, lam_ref, sub_ref, o_ref):
    qi = pl.program_id(1)
    lp = lam_ref[...]
    lam = (jnp.exp(jnp.sum(lp[0:1] * lp[1:2], axis=-1, keepdims=True))
           - jnp.exp(jnp.sum(lp[2:3] * lp[3:4], axis=-1, keepdims=True)) + lambda_init)
    lane = lax.broadcasted_iota(jnp.int32, (TQ_SUB, LANES), 1)

    def attend(n_keys):
        for h in range(DA_HEADS):
            blk, half = h // 2, h % 2
            keep = (lane // 64) == half
            k1 = k_ref[0:n_keys, blk * LANES:(blk + 1) * LANES]
            k2 = k_ref[0:n_keys, (2 + blk) * LANES:(3 + blk) * LANES]
            v = v_ref[0:n_keys, h * LANES:(h + 1) * LANES]
            for r0 in range(0, TM, TQ_SUB):
                q1 = q_ref[r0:r0 + TQ_SUB, blk * LANES:(blk + 1) * LANES]
                q2 = q_ref[r0:r0 + TQ_SUB, (2 + blk) * LANES:(3 + blk) * LANES]
                q1 = jnp.where(keep, q1, jnp.zeros_like(q1))
                q2 = jnp.where(keep, q2, jnp.zeros_like(q2))
                p1, l1 = _softmax_parts(_dot_nt(q1, k1))
                p2, l2 = _softmax_parts(_dot_nt(q2, k2))
                o = _dot(p1.astype(BF16), v) / l1 - lam * (_dot(p2.astype(BF16), v) / l2)
                o = _rms(o, sub_ref[...]) * (1.0 - lambda_init)
                o_ref[r0:r0 + TQ_SUB, h * LANES:(h + 1) * LANES] = o.astype(BF16)

    @pl.when(qi == 0)
    def _():
        attend(TM)

    @pl.when(qi != 0)
    def _():
        attend(k_ref.shape[0])


def _diff_attention(qa, ka, va, da_lambda, subln, layer, n_batch, tpb, lambda_init):
    rows = qa.shape[0]
    s = tpb * TM
    return pl.pallas_call(
        functools.partial(_da_kernel, lambda_init),
        grid=(n_batch, tpb),
        in_specs=[pl.BlockSpec((TM, 512), lambda b, i: (b * tpb + i, 0)),
                  pl.BlockSpec((None, s, 512), lambda b, i: (b, 0, 0)),
                  pl.BlockSpec((None, s, 512), lambda b, i: (b, 0, 0)),
                  pl.BlockSpec((None, 4, DA_HEAD_DIM), lambda b, i: (layer, 0, 0)),
                  pl.BlockSpec((None, 1, LANES), lambda b, i: (layer, 0, 0))],
        out_specs=pl.BlockSpec((TM, 512), lambda b, i: (b * tpb + i, 0)),
        out_shape=jax.ShapeDtypeStruct((rows, 512), BF16),
        compiler_params=_params(2),
        name="diff_attention",
    )(qa, ka.reshape(n_batch, s, 512), va.reshape(n_batch, s, 512), da_lambda, subln)


def _gqa_kernel(q_ref, k_ref, v_ref, o_ref):
    qi = pl.program_id(1)
    lane = lax.broadcasted_iota(jnp.int32, (TQ_SUB, LANES), 1)

    def attend(n_keys):
        k = k_ref[0:n_keys, :]
        v = v_ref[0:n_keys, :]
        for blk in range(GQA_HEADS // 2):
            g = (2 * blk) // GQA_HPG
            in_group = (lane // 64) == g
            for r0 in range(0, TM, TQ_SUB):
                qb = q_ref[r0:r0 + TQ_SUB, blk * LANES:(blk + 1) * LANES].astype(F32)
                qsw = pltpu.roll(qb, 64, axis=1)
                halves = []
                for half in range(2):
                    qh = qb if half == g else qsw
                    qh = jnp.where(in_group, qh, 0.0).astype(BF16)
                    p, l = _softmax_parts(_dot_nt(qh, k))
                    o = _dot(p.astype(BF16), v) / l
                    halves.append(o if half == g else pltpu.roll(o, 64, axis=1))
                o_ref[r0:r0 + TQ_SUB, blk * LANES:(blk + 1) * LANES] = jnp.where(
                    lane < 64, halves[0], halves[1]).astype(BF16)

    @pl.when(qi == 0)
    def _():
        attend(TM)

    @pl.when(qi != 0)
    def _():
        attend(k_ref.shape[0])


def _gqa_attention(gq, gk, gv, n_batch, tpb):
    rows = gq.shape[0]
    s = tpb * TM
    return pl.pallas_call(
        _gqa_kernel,
        grid=(n_batch, tpb),
        in_specs=[pl.BlockSpec((TM, 512), lambda b, i: (b * tpb + i, 0)),
                  pl.BlockSpec((None, s, LANES), lambda b, i: (b, 0, 0)),
                  pl.BlockSpec((None, s, LANES), lambda b, i: (b, 0, 0))],
        out_specs=pl.BlockSpec((TM, 512), lambda b, i: (b * tpb + i, 0)),
        out_shape=jax.ShapeDtypeStruct((rows, 512), BF16),
        compiler_params=_params(2),
        name="gqa_attention",
    )(gq, gk.reshape(n_batch, s, LANES), gv.reshape(n_batch, s, LANES))


def _ssd_direction(d, x_ref, dt_ref, alog_ref, bias_ref, skip_ref, y_ref, h_ref):
    q = TM
    lane = lax.broadcasted_iota(jnp.int32, (1, LANES), 1)
    a_vec = jnp.where(lane < 2 * SSD_HEADS, -jnp.exp(alog_ref[...]), 0.0)
    raw = dt_ref[...] + bias_ref[...]
    dt = jnp.maximum(raw, 0.0) + jnp.log1p(jnp.exp(-jnp.abs(raw)))
    da = dt * a_vec
    ri = lax.broadcasted_iota(jnp.int32, (q, q), 0)
    ci = lax.broadcasted_iota(jnp.int32, (q, q), 1)
    tri = (ci <= ri) if d == 0 else (ci >= ri)
    tri_f = tri.astype(F32)
    tri_t = ((ri <= ci) if d == 0 else (ri >= ci)).astype(F32)
    cs = jnp.dot(tri_f, da, precision=lax.Precision.HIGHEST, preferred_element_type=F32)
    cs_t = jnp.dot(da.T, tri_t, precision=lax.Precision.HIGHEST, preferred_element_type=F32)
    tot = jnp.sum(da, axis=0, keepdims=True)

    er = lax.broadcasted_iota(jnp.int32, (2 * LANES, 512), 0) % LANES
    ec = lax.broadcasted_iota(jnp.int32, (2 * LANES, 512), 1) // SSD_HEAD_DIM
    expand = (er == d * SSD_HEADS + ec).astype(BF16)
    stack = jnp.concatenate([dt, jnp.exp(tot - cs), jnp.exp(cs), jnp.broadcast_to(jnp.exp(tot), (8, LANES))], axis=0)
    hi, lo = _split2(stack)
    wide = _dot(jnp.concatenate([hi, lo], axis=1), expand)
    dt_e, dte_e, ecs_e, cdec_e = wide[0:q], wide[q:2 * q], wide[2 * q:3 * q], wide[3 * q:3 * q + 1]

    xs = x_ref[:, 0:512]
    xsdt = xs * dt_e
    lane_g = lax.broadcasted_iota(jnp.int32, (q, 2 * LANES), 1) // SSD_HEAD_DIM
    for g in range(SSD_GROUPS):
        gl = slice(g * 2 * LANES, (g + 1) * 2 * LANES)
        bm = x_ref[:, 512 + g * LANES:512 + (g + 1) * LANES]
        cm = x_ref[:, 768 + g * LANES:768 + (g + 1) * LANES].astype(BF16)
        cb = _dot_nt(cm, bm.astype(BF16))
        xg = xsdt[:, gl]
        y = _dot(cm, h_ref[g].astype(BF16)) * ecs_e[:, gl]
        for r in range(SSD_HPG):
            idx = d * SSD_HEADS + g * SSD_HPG + r
            seg = cs[:, idx:idx + 1] - cs_t[idx:idx + 1, :]
            lmat = jnp.exp(jnp.where(tri, seg, -jnp.inf))
            y = y + _dot((cb * lmat).astype(BF16), jnp.where(lane_g == r, xg, 0.0).astype(BF16))
        y_ref[:, gl] = y + skip_ref[:, gl] * xs[:, gl]
        h_ref[g] = cdec_e[:, gl] * h_ref[g] + _dot(bm.T.astype(BF16), (xg * dte_e[:, gl]).astype(BF16))


def _ssd_kernel(xf_ref, dtf_ref, xb_ref, dtb_ref, alog_ref, bias_ref, skip_ref, yf_ref, yb_ref, h_ref):
    @pl.when(pl.program_id(1) == 0)
    def _():
        h_ref[...] = jnp.zeros_like(h_ref)

    _ssd_direction(0, xf_ref, dtf_ref, alog_ref, bias_ref, skip_ref.at[0], yf_ref, h_ref.at[0])
    _ssd_direction(1, xb_ref, dtb_ref, alog_ref, bias_ref, skip_ref.at[1], yb_ref, h_ref.at[1])


def _ssd(xc, dt, alog, bias, skip, layer, n_batch, tpb):
    rows = xc.shape[0]

    def fwd(b, t):
        return (b * tpb + t, 0)

    def bwd(b, t):
        return (b * tpb + jnp.where(t == 0, 0, tpb - t), 0)

    def lyr(shape):
        return pl.BlockSpec((None,) + shape, lambda b, t: (layer,) + (0,) * len(shape))

    return pl.pallas_call(
        _ssd_kernel,
        grid=(n_batch, tpb),
        in_specs=[pl.BlockSpec((TM, 1024), fwd), pl.BlockSpec((TM, LANES), fwd),
                  pl.BlockSpec((TM, 1024), bwd), pl.BlockSpec((TM, LANES), bwd),
                  lyr((1, LANES)), lyr((1, LANES)), lyr((2, 1, 512))],
        out_specs=[pl.BlockSpec((TM, 512), fwd), pl.BlockSpec((TM, 512), bwd)],
        out_shape=[jax.ShapeDtypeStruct((rows, 512), F32)] * 2,
        scratch_shapes=[pltpu.VMEM((2, SSD_GROUPS, SSD_STATE, 2 * LANES), F32)],
        compiler_params=_params(2),
        name="ssd_scan",
    )(xc, dt, xc, dt, alog, bias, skip)


def _merge_kernel(x_ref, mod, gpre, gpost, oa, yf, yb, z, ssdn, og, cf, wg, wb, wo, o_ref):
    x = x_ref[...]
    h = (_rms(x, gpre[...]) * (1.0 + mod[1:2, :]) + mod[0:1, :]).astype(BF16)
    ys = _rms((yf[...] + yb[...]) * _silu(z[...]), ssdn[...]).astype(BF16)
    branches = [oa[...], ys, og[...], cf[...]]
    d = x.shape[-1]
    mixed = None
    for k in range(N_BRANCH):
        term = _sigmoid(_dot(h, wg[:, k * d:(k + 1) * d])) * _dot(branches[k], wb[k])
        mixed = term if mixed is None else mixed + term
    out = _dot(mixed.astype(BF16), wo[...])
    o_ref[...] = x + mod[2:3, :] * _rms(out, gpost[...])


def _merge(xs, mod, layer, tpb, n_batch, gpre, gpost, oa, yf, yb, z, ssdn, og, cf, w_gate, w_branch, w_out):
    rows, d = xs.shape

    def mod_row(t):
        return jnp.where(t % tpb == 0, n_batch, t // tpb)

    def lyr(shape):
        return pl.BlockSpec((None,) + shape, lambda t: (layer,) + (0,) * len(shape))

    def tile(width):
        return pl.BlockSpec((TM, width), lambda t: (t, 0))

    return pl.pallas_call(
        _merge_kernel,
        grid=(rows // TM,),
        in_specs=[tile(d), pl.BlockSpec((None, None, 6, d), lambda t: (layer, mod_row(t), 0, 0)),
                  lyr((1, d)), lyr((1, d)), tile(512), tile(512), tile(512), tile(512), lyr((1, BRANCH_W)),
                  tile(512), tile(512), lyr((d, N_BRANCH * d)), lyr((N_BRANCH, BRANCH_W, d)), lyr((d, d))],
        out_specs=tile(d),
        out_shape=jax.ShapeDtypeStruct((rows, d), F32),
        compiler_params=_params(1),
        name="merge",
    )(xs, mod, gpre, gpost, oa, yf, yb, z, ssdn, og, cf, w_gate, w_branch, w_out)


def _ffn_chunks(d_ff):
    chunks, c0 = [], 0
    while c0 < d_ff:
        cw = min(1024, d_ff - c0)
        chunks.append((c0, cw))
        c0 += cw
    return chunks


def _ffn_kernel(tiles_per_batch, d_ff, xm, xp, xn, mod, gpre, gpost, wu, cw_ref, cb_ref, wd, o_ref, abuf, gbuf):
    ti = pl.program_id(0) % tiles_per_batch
    prev_ok = ti >= 2
    next_ok = jnp.logical_and(ti >= 1, ti <= tiles_per_batch - 2)
    shift, scale, gain = mod[3:4, :], mod[4:5, :], gpre[...]

    def normmod(v):
        return _rms(v, gain) * (1.0 + scale) + shift

    x = xm[...]
    hext = jnp.concatenate([jnp.where(prev_ok, normmod(xp[...]), 0.0), normmod(x),
                            jnp.where(next_ok, normmod(xn[...]), 0.0)], axis=0).astype(BF16)

    def conv(buf, c0, cw):
        acc = cb_ref[:, c0:c0 + cw] + cw_ref[0:1, c0:c0 + cw] * buf[pl.ds(HALO_F - 1, TM), 0:cw]
        for k in range(1, FFN_CONV):
            acc = acc + cw_ref[k:k + 1, c0:c0 + cw] * buf[pl.ds(HALO_F - 1 + k, TM), 0:cw]
        return acc

    out = None
    for c0, cw in _ffn_chunks(d_ff):
        abuf[:, 0:cw] = _dot(hext, wu[:, c0:c0 + cw])
        gbuf[:, 0:cw] = _dot(hext, wu[:, d_ff + c0:d_ff + c0 + cw])
        act = (_silu(conv(gbuf, d_ff + c0, cw)) * conv(abuf, c0, cw)).astype(BF16)
        part = _dot(act, wd[c0:c0 + cw, :])
        out = part if out is None else out + part
    o_ref[...] = x + mod[5:6, :] * _rms(out, gpost[...])


def _ffn(xs, mod, layer, tpb, n_batch, gpre, gpost, w_up, conv_w, conv_b, w_down):
    rows, d = xs.shape
    d_ff = w_down.shape[1]
    hb = TM // HALO_F
    n_hblk = rows // HALO_F

    def mod_row(t):
        return jnp.where(t % tpb == 0, n_batch, t // tpb)

    def lyr(shape):
        return pl.BlockSpec((None,) + shape, lambda t: (layer,) + (0,) * len(shape))

    return pl.pallas_call(
        functools.partial(_ffn_kernel, tpb, d_ff),
        grid=(rows // TM,),
        in_specs=[pl.BlockSpec((TM, d), lambda t: (t, 0)),
                  pl.BlockSpec((HALO_F, d), lambda t: (jnp.maximum(t * hb - 1, 0), 0)),
                  pl.BlockSpec((HALO_F, d), lambda t: (jnp.minimum((t + 1) * hb, n_hblk - 1), 0)),
                  pl.BlockSpec((None, None, 6, d), lambda t: (layer, mod_row(t), 0, 0)),
                  lyr((1, d)), lyr((1, d)), lyr((d, 2 * d_ff)), lyr((FFN_CONV, 2 * d_ff)), lyr((1, 2 * d_ff)),
                  lyr((d_ff, d))],
        out_specs=pl.BlockSpec((TM, d), lambda t: (t, 0)),
        out_shape=jax.ShapeDtypeStruct((rows, d), F32),
        scratch_shapes=[pltpu.VMEM((TM + 2 * HALO_F, 1024), F32)] * 2,
        compiler_params=_params(1),
        name="conv_ffn",
    )(xs, xs, xs, mod, gpre, gpost, w_up, conv_w, conv_b, w_down)


def _rope_tables(seq):
    rows = seq // GRID_W
    row = jnp.repeat(jnp.arange(rows, dtype=F32), GRID_W)
    col = jnp.tile(jnp.arange(GRID_W, dtype=F32), rows)
    quarter = DA_HEAD_DIM // 4
    inv_freq = ROPE_BASE ** (-jnp.arange(quarter, dtype=F32) / quarter)
    ang = jnp.concatenate([row[:, None] * inv_freq, col[:, None] * inv_freq], axis=-1)
    cos, sin = jnp.cos(ang), jnp.sin(ang)
    cos_t = jnp.concatenate([cos] * 4, axis=-1)
    sin_t = jnp.concatenate([-sin, sin, -sin, sin], axis=-1)
    cos_t = jnp.concatenate([jnp.ones((TM, LANES), F32), cos_t], axis=0)
    sin_t = jnp.concatenate([jnp.zeros((TM, LANES), F32), sin_t], axis=0)
    return cos_t, sin_t


def kernel(x, c, ctx, c_ctx, w_mod, b_mod, mix_norm_pre, mix_norm_post, w_in, da_lambda, da_subln, ssd_conv_w, ssd_conv_b, ssd_a_log, ssd_dt_bias, ssd_d, ssd_norm, gqa_q_norm, gqa_k_norm, conf_conv_w, conf_conv_b, conf_ln_g, conf_ln_b, w_branch, w_out, ffn_norm_pre, ffn_norm_post, ffn_w_up, ffn_conv_w, ffn_conv_b, ffn_w_down):
    n_batch, seq, d = x.shape
    depth = w_mod.shape[0]
    assert ctx.shape[1] == TM and seq % TM == 0 and seq % GRID_W == 0
    tpb = (TM + seq) // TM
    rows = n_batch * tpb * TM

    o_dt = C_XBC + 1024
    o_gate = w_in.shape[-1] - N_BRANCH * d
    w_a = jnp.concatenate([w_in[:, :, :o_dt], w_in[:, :, o_dt + 2 * SSD_HEADS:o_gate],
                           w_in[:, :, o_dt:o_dt + 2 * SSD_HEADS],
                           jnp.zeros((depth, d, C_END - C_DT - 2 * SSD_HEADS), F32)], axis=-1).astype(BF16)
    w_gate = w_in[:, :, o_gate:].astype(BF16)
    w_br = w_branch.astype(BF16)
    w_o = w_out.astype(BF16)
    w_up = ffn_w_up.astype(BF16)
    w_dn = ffn_w_down.astype(BF16)

    def row3(p):
        return p.reshape(depth, 1, -1)

    def pad_lanes(p):
        flat = p.reshape(depth, 1, -1)
        return jnp.pad(flat, ((0, 0), (0, 0), (0, LANES - flat.shape[-1])))

    qn = row3(jnp.concatenate([gqa_q_norm, gqa_q_norm], axis=-1))
    kn = row3(jnp.concatenate([gqa_k_norm, gqa_k_norm], axis=-1))
    skip = jnp.repeat(ssd_d, SSD_HEAD_DIM, axis=-1).reshape(depth, 2, 1, SSD_HEADS * SSD_HEAD_DIM)
    cos_t, sin_t = _rope_tables(seq)

    cvec = jnp.concatenate([c, c_ctx[None, :], jnp.zeros((8 - n_batch - 1, d), F32)], axis=0)
    mod = _modulation(cvec, w_mod, b_mod).reshape(depth, 8, 6, d)

    xs = jnp.concatenate([ctx, x], axis=1).reshape(rows, d)
    for l in range(depth):
        lambda_init = 0.8 - 0.6 * math.exp(-0.3 * l)
        qa, ka, va, z, xc, dt, gq, gk, gv, cf = _inproj(
            xs, mod, l, tpb, n_batch, row3(mix_norm_pre), w_a, cos_t, sin_t, qn, kn,
            ssd_conv_w, row3(ssd_conv_b), conf_conv_w, row3(conf_conv_b), row3(conf_ln_g), row3(conf_ln_b))
        oa = _diff_attention(qa, ka, va, da_lambda, row3(da_subln), l, n_batch, tpb, lambda_init)
        og = _gqa_attention(gq, gk, gv, n_batch, tpb)
        yf, yb = _ssd(xc, dt, pad_lanes(ssd_a_log), pad_lanes(ssd_dt_bias), skip, l, n_batch, tpb)
        xs = _merge(xs, mod, l, tpb, n_batch, row3(mix_norm_pre), row3(mix_norm_post), oa, yf, yb, z,
                    row3(ssd_norm), og, cf, w_gate, w_br, w_o)
        xs = _ffn(xs, mod, l, tpb, n_batch, row3(ffn_norm_pre), row3(ffn_norm_post), w_up, ffn_conv_w,
                  row3(ffn_conv_b), w_dn)
    return xs.reshape(n_batch, tpb * TM, d)[:, TM:, :]
```

```python
import functools
import math

import jax
import jax.numpy as jnp
from jax import lax
from jax.experimental import pallas as pl
from jax.experimental.pallas import tpu as pltpu

F32 = jnp.float32
BF16 = jnp.bfloat16

GRID_W = 64
ROPE_BASE = 10000.0
EPS = 1e-6
N_BRANCH = 4
BRANCH_W = 512
DA_HEADS = 4
DA_HEAD_DIM = 64
SSD_HEADS = 8
SSD_HEAD_DIM = 64
SSD_GROUPS = 2
SSD_HPG = SSD_HEADS // SSD_GROUPS
SSD_STATE = 128
SSD_CONV = 5
GQA_HEADS = 8
GQA_KV_HEADS = 2
GQA_HPG = GQA_HEADS // GQA_KV_HEADS
GQA_HEAD_DIM = 64
CONF_CONV = 31
FFN_CONV = 3

LANES = 128
SUBLANES = 8
TM_LATENT = 512
CHUNK = 256
HALO_A = 16
HALO_F = 8
TQ = 1024
KEY_CHUNK = 1024
DA_SUB = 512
GQA_SUB = 256
VMEM_LIMIT = 56 * 1024 * 1024

C_DAQ, C_DAK, C_DAV, C_Z, C_XBC, C_GQ, C_GK, C_GV, C_CONF, C_DT, C_END = (
    0, 512, 1024, 1536, 2048, 3072, 3584, 3712, 3840, 4864, 4992)
LOG2E = math.log2(math.e)


def _params(n_axes):
    return pltpu.CompilerParams(dimension_semantics=("arbitrary",) * n_axes, vmem_limit_bytes=VMEM_LIMIT)


def _resident(shape, index_map):
    return pl.BlockSpec(shape, index_map, pipeline_mode=pl.Buffered(1))


def _sigmoid(v):
    return 1.0 / (1.0 + jnp.exp(-v))


def _silu(v):
    return v * _sigmoid(v)


def _rms(v, gain):
    return v * lax.rsqrt(jnp.mean(v * v, axis=-1, keepdims=True) + EPS) * gain


def _dot(a, b):
    return jnp.dot(a, b, preferred_element_type=F32)


def _dot_nt(a, b):
    return lax.dot_general(a, b, (((1,), (1,)), ((), ())), preferred_element_type=F32)


def _split2(v):
    hi = v.astype(BF16)
    lo = (v - hi.astype(F32)).astype(BF16)
    return hi, lo


def _mod_kernel(c_ref, w_ref, b_ref, o_ref):
    s = _silu(c_ref[...])
    hi, lo = _split2(s)
    lhs = jnp.concatenate([hi, lo], axis=0)
    r = _dot(lhs, w_ref[...].astype(BF16))
    o_ref[...] = r[0:8] + r[8:16] + b_ref[...]


def _modulation(cvec, w_mod, b_mod):
    depth, d, n = w_mod.shape
    tn = 1536
    return pl.pallas_call(
        _mod_kernel,
        grid=(depth, n // tn),
        in_specs=[pl.BlockSpec((8, d), lambda l, j: (0, 0)),
                  pl.BlockSpec((None, d, tn), lambda l, j: (l, 0, j)),
                  pl.BlockSpec((None, 1, tn), lambda l, j: (l, 0, j))],
        out_specs=pl.BlockSpec((None, 8, tn), lambda l, j: (l, 0, j)),
        out_shape=jax.ShapeDtypeStruct((depth, 8, n), F32),
        compiler_params=_params(2),
        name="modulation",
    )(cvec, w_mod, b_mod.reshape(depth, 1, n))


def _row_specs(rows, d, tm, halo):
    hb = tm // halo
    last = rows // halo - 1
    return [pl.BlockSpec((tm, d), lambda t: (t, 0)),
            pl.BlockSpec((halo, d), lambda t: (jnp.maximum(t * hb - 1, 0), 0)),
            pl.BlockSpec((halo, d), lambda t: (jnp.minimum((t + 1) * hb, last), 0))]


def _mod_spec(layer, d, mod_row):
    return pl.BlockSpec((None, None, 6, d), lambda t: (layer, mod_row(t), 0, 0))


def _layer_spec(layer, shape):
    return _resident((None,) + shape, lambda t: (layer,) + (0,) * len(shape))


def _rope128(tb, cos, sin, low32):
    up = pltpu.roll(tb, LANES - 32, axis=1)
    dn = pltpu.roll(tb, 32, axis=1)
    return tb * cos + jnp.where(low32, up, dn) * sin


def _inproj_kernel(tm, tiles_per_seg, use_rope, *refs):
    if use_rope:
        (xm, xp, xn, mod, gpre, w, qn, kn, scw, scb, ccw, ccb, lng, lnb, cos_r, sin_r,
         qa_o, kat_o, va_o, z_o, xc_o, dt_o, gq_o, gkt_o, gv_o, cf_o, sbuf, rbuf) = refs
    else:
        (xm, xp, xn, mod, gpre, w, qn, kn, scw, scb, ccw, ccb, lng, lnb,
         qa_o, kat_o, va_o, z_o, xc_o, dt_o, gq_o, gkt_o, gv_o, cf_o, sbuf, rbuf) = refs
    ti = pl.program_id(0) % tiles_per_seg
    prev_ok = ti > 0
    next_ok = ti < tiles_per_seg - 1
    shift, scale, gain = mod[0:1, :], mod[1:2, :], gpre[...]

    def normmod(v):
        return _rms(v, gain) * (1.0 + scale) + shift

    hm = normmod(xm[...]).astype(BF16)
    hp = jnp.where(prev_ok, normmod(xp[...]), 0.0).astype(BF16)
    hn = jnp.where(next_ok, normmod(xn[...]), 0.0).astype(BF16)
    hext = jnp.concatenate([hp, hm, hn], axis=0)

    lane = lax.broadcasted_iota(jnp.int32, (tm, LANES), 1)
    low32 = (lane % 64) < 32
    gi = lax.broadcasted_iota(jnp.int32, (LANES, LANES), 0) // 64
    gj = lax.broadcasted_iota(jnp.int32, (LANES, LANES), 1) // 64
    gsum = (gi == gj).astype(BF16)

    def rope(tb):
        return _rope128(tb, cos_r[...], sin_r[...], low32) if use_rope else tb

    def headnorm(tb, g):
        ss = _dot((tb * tb).astype(BF16), gsum)
        return tb * lax.rsqrt(ss * (1.0 / GQA_HEAD_DIM) + EPS) * g

    def halves(c0):
        r = _dot(hm, w[:, c0:c0 + 2 * LANES])
        return r[:, 0:LANES], r[:, LANES:2 * LANES]

    for j in range(2):
        for i, q in enumerate(halves(C_DAQ + j * 2 * LANES)):
            blk = 2 * j + i
            qa_o[:, blk * LANES:(blk + 1) * LANES] = (rope(q) * (DA_HEAD_DIM ** -0.5 * LOG2E)).astype(BF16)
        for i, k in enumerate(halves(C_DAK + j * 2 * LANES)):
            blk = 2 * j + i
            kat_o[blk * LANES:(blk + 1) * LANES, :] = rope(k).T.astype(BF16)
    va_o[...] = _dot(hm, w[:, C_DAV:C_Z]).astype(BF16)
    z_o[...] = _dot(hm, w[:, C_Z:C_XBC])
    dt_o[...] = _dot(hm, w[:, C_DT:C_END])

    for j in range(2):
        for i, q in enumerate(halves(C_GQ + j * 2 * LANES)):
            blk = 2 * j + i
            q = rope(headnorm(q, qn[...]))
            gq_o[:, blk * LANES:(blk + 1) * LANES] = (q * (GQA_HEAD_DIM ** -0.5 * LOG2E)).astype(BF16)
    k, v = halves(C_GK)
    gkt_o[...] = rope(headnorm(k, kn[...])).T.astype(BF16)
    gv_o[...] = v.astype(BF16)

    sbuf[...] = _dot(hext, w[:, C_XBC:C_GQ])
    acc = scb[...] + scw[0:1, :] * sbuf[pl.ds(HALO_A - SSD_CONV // 2, tm), :]
    for k_tap in range(1, SSD_CONV):
        acc = acc + scw[k_tap:k_tap + 1, :] * sbuf[pl.ds(HALO_A - SSD_CONV // 2 + k_tap, tm), :]
    xc_o[...] = _silu(acc)

    a = _dot(hext, w[:, C_CONF:C_CONF + BRANCH_W])
    g = _dot(hext, w[:, C_CONF + BRANCH_W:C_DT])
    rbuf[0] = a * _sigmoid(g)
    span = tm + 2 * HALO_A - SUBLANES
    for r in range(1, SUBLANES):
        rbuf[r, 0:span, :] = rbuf[0, pl.ds(r, span), :]
    acc = ccb[...]
    for k_tap in range(CONF_CONV):
        off = HALO_A - CONF_CONV // 2 + k_tap
        acc = acc + ccw[k_tap:k_tap + 1, :] * rbuf[off % SUBLANES, pl.ds(off - off % SUBLANES, tm), :]
    xc = acc - jnp.mean(acc, axis=-1, keepdims=True)
    y = xc * lax.rsqrt(jnp.mean(xc * xc, axis=-1, keepdims=True) + EPS) * lng[...] + lnb[...]
    cf_o[...] = _silu(y).astype(BF16)


def _inproj(xs, seg_len, tm, layer, mod, mod_row, rope, gpre, w_a, qn, kn, scw, scb, ccw, ccb, lng, lnb):
    rows, d = xs.shape
    tps = seg_len // tm
    n_seg = rows // seg_len

    def out(width, dtype):
        return pl.BlockSpec((tm, width), lambda t: (t, 0)), jax.ShapeDtypeStruct((rows, width), dtype)

    def out_t(width):
        return (pl.BlockSpec((None, width, tm), lambda t: (t // tps, 0, t % tps)),
                jax.ShapeDtypeStruct((n_seg, width, seg_len), BF16))

    outs = [out(512, BF16), out_t(512), out(512, BF16), out(512, F32), out(1024, F32), out(LANES, F32),
            out(512, BF16), out_t(LANES), out(LANES, BF16), out(512, BF16)]
    in_specs = _row_specs(rows, d, tm, HALO_A) + [
        _mod_spec(layer, d, mod_row), _layer_spec(layer, (1, d)), _layer_spec(layer, (d, C_END)),
        _layer_spec(layer, (1, LANES)), _layer_spec(layer, (1, LANES)),
        _layer_spec(layer, (SSD_CONV, 1024)), _layer_spec(layer, (1, 1024)),
        _layer_spec(layer, (CONF_CONV, BRANCH_W)), _layer_spec(layer, (1, BRANCH_W)),
        _layer_spec(layer, (1, BRANCH_W)), _layer_spec(layer, (1, BRANCH_W))]
    args = [xs, xs, xs, mod, gpre, w_a, qn, kn, scw, scb, ccw, ccb, lng, lnb]
    if rope is not None:
        in_specs += [pl.BlockSpec((tm, LANES), lambda t: (t % tps, 0))] * 2
        args += list(rope)
    return pl.pallas_call(
        functools.partial(_inproj_kernel, tm, tps, rope is not None),
        grid=(rows // tm,),
        in_specs=in_specs,
        out_specs=[o[0] for o in outs],
        out_shape=[o[1] for o in outs],
        scratch_shapes=[pltpu.VMEM((tm + 2 * HALO_A, 1024), F32),
                        pltpu.VMEM((SUBLANES, tm + 2 * HALO_A, BRANCH_W), F32)],
        compiler_params=_params(1),
        name="inproj",
    )(*args)


def _key_chunks(seg_lens):
    return [(s, off, min(KEY_CHUNK, n - off)) for s, n in enumerate(seg_lens) for off in range(0, n, KEY_CHUNK)]


def _online_softmax_pv(score_fn, value_fn, chunks):
    m, acc = None, None
    for c in chunks:
        s = score_fn(c)
        mc = jnp.max(s, axis=-1, keepdims=True)
        if m is None:
            m = mc
            acc = _dot(jnp.exp2(s - m).astype(BF16), value_fn(c))
        else:
            m_new = jnp.maximum(m, mc)
            acc = acc * jnp.exp2(m - m_new) + _dot(jnp.exp2(s - m_new).astype(BF16), value_fn(c))
            m = m_new
    return acc


def _da_kernel(lambda_init, sub, seg_lens, q_ref, *refs):
    n_seg = len(seg_lens)
    kt_refs, v_refs = refs[0:2 * n_seg:2], refs[1:2 * n_seg:2]
    lam_ref, sub_ref, o_ref = refs[2 * n_seg:]
    chunks = _key_chunks(seg_lens)
    lp = lam_ref[...]
    lam = (jnp.exp(jnp.sum(lp[0:1] * lp[1:2], axis=-1, keepdims=True))
           - jnp.exp(jnp.sum(lp[2:3] * lp[3:4], axis=-1, keepdims=True)) + lambda_init)
    lane = lax.broadcasted_iota(jnp.int32, (sub, LANES), 1)

    def tile(i, carry):
        rs = pl.ds(pl.multiple_of(i * sub, sub), sub)
        for h in range(DA_HEADS):
            blk, half = h // 2, h % 2
            keep = (lane // 64) == half
            q1 = q_ref[rs, blk * LANES:(blk + 1) * LANES]
            q2 = q_ref[rs, (2 + blk) * LANES:(3 + blk) * LANES]
            q1 = jnp.where(keep, q1, jnp.zeros_like(q1))
            q2 = jnp.where(keep, q2, jnp.zeros_like(q2))

            def scores(c):
                s, off, n = c
                s1 = _dot(q1, kt_refs[s][blk * LANES:(blk + 1) * LANES, off:off + n])
                s2 = _dot(q2, kt_refs[s][(2 + blk) * LANES:(3 + blk) * LANES, off:off + n])
                return jnp.concatenate([s1, s2], axis=0)

            def values(c):
                s, off, n = c
                v = v_refs[s][off:off + n, h * LANES:(h + 1) * LANES]
                return jnp.concatenate([v, jnp.ones_like(v)], axis=1)

            acc = _online_softmax_pv(scores, values, chunks)
            o1 = acc[0:sub, 0:LANES] / acc[0:sub, LANES:2 * LANES]
            o2 = acc[sub:2 * sub, 0:LANES] / acc[sub:2 * sub, LANES:2 * LANES]
            o = _rms(o1 - lam * o2, sub_ref[...]) * (1.0 - lambda_init)
            o_ref[rs, h * LANES:(h + 1) * LANES] = o.astype(BF16)
        return carry

    lax.fori_loop(0, q_ref.shape[0] // sub, tile, 0)


def _diff_attention(q, tq, segs, da_lambda, subln, layer, lambda_init):
    rows = q.shape[0]
    n_batch = segs[0][0].shape[0]
    nq = rows // n_batch // tq
    seg_lens = tuple(kt.shape[2] for kt, _ in segs)
    in_specs = [pl.BlockSpec((tq, 512), lambda b, i: (b * nq + i, 0))]
    args = [q]
    for kt, v in segs:
        n = kt.shape[2]
        in_specs += [pl.BlockSpec((None, 512, n), lambda b, i: (b, 0, 0)),
                     pl.BlockSpec((None, n, 512), lambda b, i: (b, 0, 0))]
        args += [kt, v.reshape(n_batch, n, 512)]
    in_specs += [pl.BlockSpec((None, 4, DA_HEAD_DIM), lambda b, i: (layer, 0, 0)),
                 pl.BlockSpec((None, 1, LANES), lambda b, i: (layer, 0, 0))]
    return pl.pallas_call(
        functools.partial(_da_kernel, lambda_init, min(DA_SUB, tq), seg_lens),
        grid=(n_batch, nq),
        in_specs=in_specs,
        out_specs=pl.BlockSpec((tq, 512), lambda b, i: (b * nq + i, 0)),
        out_shape=jax.ShapeDtypeStruct((rows, 512), BF16),
        compiler_params=_params(2),
        name="diff_attention",
    )(*args, da_lambda, subln)


def _gqa_kernel(sub, seg_lens, q_ref, *refs):
    n_seg = len(seg_lens)
    kt_refs, v_refs = refs[0:2 * n_seg:2], refs[1:2 * n_seg:2]
    o_ref = refs[2 * n_seg]
    chunks = _key_chunks(seg_lens)
    lane = lax.broadcasted_iota(jnp.int32, (sub, LANES), 1)
    lane_st = lax.broadcasted_iota(jnp.int32, (GQA_HPG * sub, LANES), 1)

    def tile(i, carry):
        rs = pl.ds(pl.multiple_of(i * sub, sub), sub)
        for g in range(GQA_KV_HEADS):
            in_group = (lane // 64) == g
            stack = []
            for r in range(GQA_HPG):
                head = g * GQA_HPG + r
                qb = q_ref[rs, (head // 2) * LANES:(head // 2 + 1) * LANES].astype(F32)
                if head % 2 != g:
                    qb = pltpu.roll(qb, 64, axis=1)
                stack.append(jnp.where(in_group, qb, 0.0).astype(BF16))
            qst = jnp.concatenate(stack, axis=0)

            def scores(c):
                s, off, n = c
                return _dot(qst, kt_refs[s][:, off:off + n])

            def values(c):
                s, off, n = c
                v = v_refs[s][off:off + n, :]
                vl = lax.broadcasted_iota(jnp.int32, v.shape, 1)
                return jnp.where((vl // 64) == g, v, jnp.ones_like(v))

            acc = _online_softmax_pv(scores, values, chunks)
            den = jnp.where((lane_st // 64) == g, pltpu.roll(acc, 64, axis=1), 1.0)
            o = acc / den
            for pair in range(GQA_HPG // 2):
                pieces = []
                for half in range(2):
                    r = 2 * pair + half
                    piece = o[r * sub:(r + 1) * sub]
                    pieces.append(piece if half == g else pltpu.roll(piece, 64, axis=1))
                blk = (g * GQA_HPG) // 2 + pair
                o_ref[rs, blk * LANES:(blk + 1) * LANES] = jnp.where(lane < 64, pieces[0], pieces[1]).astype(BF16)
        return carry

    lax.fori_loop(0, q_ref.shape[0] // sub, tile, 0)


def _gqa_attention(q, tq, segs):
    rows = q.shape[0]
    n_batch = segs[0][0].shape[0]
    nq = rows // n_batch // tq
    seg_lens = tuple(kt.shape[2] for kt, _ in segs)
    in_specs = [pl.BlockSpec((tq, 512), lambda b, i: (b * nq + i, 0))]
    args = [q]
    for kt, v in segs:
        n = kt.shape[2]
        in_specs += [pl.BlockSpec((None, LANES, n), lambda b, i: (b, 0, 0)),
                     pl.BlockSpec((None, n, LANES), lambda b, i: (b, 0, 0))]
        args += [kt, v.reshape(n_batch, n, LANES)]
    return pl.pallas_call(
        functools.partial(_gqa_kernel, min(GQA_SUB, tq), seg_lens),
        grid=(n_batch, nq),
        in_specs=in_specs,
        out_specs=pl.BlockSpec((tq, 512), lambda b, i: (b * nq + i, 0)),
        out_shape=jax.ShapeDtypeStruct((rows, 512), BF16),
        compiler_params=_params(2),
        name="gqa_attention",
    )(*args)


def _ssd_direction(d, x_ref, dt_ref, alog_ref, bias_ref, skip_ref, y_ref, h_ref):
    q = CHUNK
    lane = lax.broadcasted_iota(jnp.int32, (1, LANES), 1)
    a_vec = jnp.where(lane < 2 * SSD_HEADS, -jnp.exp(alog_ref[...]), 0.0)
    raw = dt_ref[...] + bias_ref[...]
    dt = jnp.maximum(raw, 0.0) + jnp.log1p(jnp.exp(-jnp.abs(raw)))
    da = dt * a_vec
    ri = lax.broadcasted_iota(jnp.int32, (q, q), 0)
    ci = lax.broadcasted_iota(jnp.int32, (q, q), 1)
    tri = (ci <= ri) if d == 0 else (ci >= ri)
    tri_f = tri.astype(F32)
    tri_t = ((ri <= ci) if d == 0 else (ri >= ci)).astype(F32)
    cs = jnp.dot(tri_f, da, precision=lax.Precision.HIGHEST, preferred_element_type=F32)
    cs_t = jnp.dot(da.T, tri_t, precision=lax.Precision.HIGHEST, preferred_element_type=F32)
    tot = jnp.sum(da, axis=0, keepdims=True)

    er = lax.broadcasted_iota(jnp.int32, (2 * LANES, 512), 0) % LANES
    ec = lax.broadcasted_iota(jnp.int32, (2 * LANES, 512), 1) // SSD_HEAD_DIM
    expand = (er == d * SSD_HEADS + ec).astype(BF16)
    stack = jnp.concatenate([dt, jnp.exp(tot - cs), jnp.exp(cs), jnp.broadcast_to(jnp.exp(tot), (8, LANES))], axis=0)
    hi, lo = _split2(stack)
    wide = _dot(jnp.concatenate([hi, lo], axis=1), expand)
    dt_e, dte_e, ecs_e, cdec_e = wide[0:q], wide[q:2 * q], wide[2 * q:3 * q], wide[3 * q:3 * q + 1]

    xs = x_ref[:, 0:512]
    xsdt = xs * dt_e
    lane_g = lax.broadcasted_iota(jnp.int32, (q, 2 * LANES), 1) // SSD_HEAD_DIM
    for g in range(SSD_GROUPS):
        gl = slice(g * 2 * LANES, (g + 1) * 2 * LANES)
        bm = x_ref[:, 512 + g * LANES:512 + (g + 1) * LANES]
        cm = x_ref[:, 768 + g * LANES:768 + (g + 1) * LANES].astype(BF16)
        cb = _dot_nt(cm, bm.astype(BF16))
        xg = xsdt[:, gl]
        y = _dot(cm, h_ref[g].astype(BF16)) * ecs_e[:, gl]
        for r in range(SSD_HPG):
            idx = d * SSD_HEADS + g * SSD_HPG + r
            seg = cs[:, idx:idx + 1] - cs_t[idx:idx + 1, :]
            lmat = jnp.exp(jnp.where(tri, seg, -jnp.inf))
            y = y + _dot((cb * lmat).astype(BF16), jnp.where(lane_g == r, xg, 0.0).astype(BF16))
        y_ref[:, gl] = y + skip_ref[:, gl] * xs[:, gl]
        h_ref[g] = cdec_e[:, gl] * h_ref[g] + _dot(bm.T.astype(BF16), (xg * dte_e[:, gl]).astype(BF16))


def _ssd_kernel(xf, dtf, xb, dtb, xc, dtc, alog, bias, skip, yf, yb, yfc, ybc, h_ref):
    t = pl.program_id(1)

    @pl.when(t == 0)
    def _():
        h_ref[...] = jnp.zeros_like(h_ref)
        yf[...] = jnp.zeros_like(yf)
        yb[...] = jnp.zeros_like(yb)
        _ssd_direction(0, xc, dtc, alog, bias, skip.at[0], yfc, h_ref.at[0])
        _ssd_direction(1, xc, dtc, alog, bias, skip.at[1], ybc, h_ref.at[1])

    @pl.when(t != 0)
    def _():
        _ssd_direction(0, xf, dtf, alog, bias, skip.at[0], yf, h_ref.at[0])
        _ssd_direction(1, xb, dtb, alog, bias, skip.at[1], yb, h_ref.at[1])


def _ssd(xc_l, dt_l, xc_c, dt_c, alog, bias, skip, layer, n_batch):
    rows_l, rows_c = xc_l.shape[0], xc_c.shape[0]
    n_l = rows_l // n_batch // CHUNK

    def fwd(b, t):
        return (b * n_l + jnp.maximum(t - 1, 0), 0)

    def bwd(b, t):
        return (b * n_l + n_l - jnp.maximum(t, 1), 0)

    def cxt(b, t):
        return (b, 0)

    def lyr(shape):
        return pl.BlockSpec((None,) + shape, lambda b, t: (layer,) + (0,) * len(shape))

    return pl.pallas_call(
        _ssd_kernel,
        grid=(n_batch, n_l + 1),
        in_specs=[pl.BlockSpec((CHUNK, 1024), fwd), pl.BlockSpec((CHUNK, LANES), fwd),
                  pl.BlockSpec((CHUNK, 1024), bwd), pl.BlockSpec((CHUNK, LANES), bwd),
                  pl.BlockSpec((CHUNK, 1024), cxt), pl.BlockSpec((CHUNK, LANES), cxt),
                  lyr((1, LANES)), lyr((1, LANES)), lyr((2, 1, 512))],
        out_specs=[pl.BlockSpec((CHUNK, 512), fwd), pl.BlockSpec((CHUNK, 512), bwd),
                   pl.BlockSpec((CHUNK, 512), cxt), pl.BlockSpec((CHUNK, 512), cxt)],
        out_shape=[jax.ShapeDtypeStruct((rows_l, 512), F32)] * 2 + [jax.ShapeDtypeStruct((rows_c, 512), F32)] * 2,
        scratch_shapes=[pltpu.VMEM((2, SSD_GROUPS, SSD_STATE, 2 * LANES), F32)],
        compiler_params=_params(2),
        name="ssd_scan",
    )(xc_l, dt_l, xc_l, dt_l, xc_c, dt_c, alog, bias, skip)


def _merge_kernel(x_ref, mod, gpre, gpost, oa, yf, yb, z, ssdn, og, cf, wg, wb, wo, o_ref):
    x = x_ref[...]
    h = (_rms(x, gpre[...]) * (1.0 + mod[1:2, :]) + mod[0:1, :]).astype(BF16)
    ys = _rms((yf[...] + yb[...]) * _silu(z[...]), ssdn[...]).astype(BF16)
    branches = [oa[...], ys, og[...], cf[...]]
    d = x.shape[-1]
    mixed = None
    for k in range(N_BRANCH):
        term = _sigmoid(_dot(h, wg[:, k * d:(k + 1) * d])) * _dot(branches[k], wb[k])
        mixed = term if mixed is None else mixed + term
    out = _dot(mixed.astype(BF16), wo[...])
    o_ref[...] = x + mod[2:3, :] * _rms(out, gpost[...])


def _merge(xs, tm, layer, mod, mod_row, gpre, gpost, oa, yf, yb, z, ssdn, og, cf, w_gate, w_branch, w_out):
    rows, d = xs.shape

    def tile(width):
        return pl.BlockSpec((tm, width), lambda t: (t, 0))

    return pl.pallas_call(
        _merge_kernel,
        grid=(rows // tm,),
        in_specs=[tile(d), _mod_spec(layer, d, mod_row), _layer_spec(layer, (1, d)), _layer_spec(layer, (1, d)),
                  tile(512), tile(512), tile(512), tile(512), _layer_spec(layer, (1, BRANCH_W)), tile(512), tile(512),
                  _layer_spec(layer, (d, N_BRANCH * d)), _layer_spec(layer, (N_BRANCH, BRANCH_W, d)),
                  _layer_spec(layer, (d, d))],
        out_specs=tile(d),
        out_shape=jax.ShapeDtypeStruct((rows, d), F32),
        compiler_params=_params(1),
        name="merge",
    )(xs, mod, gpre, gpost, oa, yf, yb, z, ssdn, og, cf, w_gate, w_branch, w_out)


FFN_COL_CHUNK = 512


def _ffn_chunks(d_ff):
    return [(c0, min(FFN_COL_CHUNK, d_ff - c0)) for c0 in range(0, d_ff, FFN_COL_CHUNK)]


def _ffn_kernel(tm, tiles_per_seg, d_ff, xm, xp, xn, mod, gpre, gpost, wu, cw_ref, cb_ref, wd, o_ref, abuf, gbuf):
    ti = pl.program_id(0) % tiles_per_seg
    prev_ok = ti > 0
    next_ok = ti < tiles_per_seg - 1
    shift, scale, gain = mod[3:4, :], mod[4:5, :], gpre[...]

    def normmod(v):
        return _rms(v, gain) * (1.0 + scale) + shift

    x = xm[...]
    hext = jnp.concatenate([jnp.where(prev_ok, normmod(xp[...]), 0.0), normmod(x),
                            jnp.where(next_ok, normmod(xn[...]), 0.0)], axis=0).astype(BF16)

    def conv(buf, c0, cw):
        acc = cb_ref[:, c0:c0 + cw] + cw_ref[0:1, c0:c0 + cw] * buf[pl.ds(HALO_F - 1, tm), 0:cw]
        for k in range(1, FFN_CONV):
            acc = acc + cw_ref[k:k + 1, c0:c0 + cw] * buf[pl.ds(HALO_F - 1 + k, tm), 0:cw]
        return acc

    out = None
    for c0, cw in _ffn_chunks(d_ff):
        abuf[:, 0:cw] = _dot(hext, wu[:, c0:c0 + cw])
        gbuf[:, 0:cw] = _dot(hext, wu[:, d_ff + c0:d_ff + c0 + cw])
        act = (_silu(conv(gbuf, d_ff + c0, cw)) * conv(abuf, c0, cw)).astype(BF16)
        part = _dot(act, wd[c0:c0 + cw, :])
        out = part if out is None else out + part
    o_ref[...] = x + mod[5:6, :] * _rms(out, gpost[...])


def _ffn(xs, seg_len, tm, layer, mod, mod_row, gpre, gpost, w_up, conv_w, conv_b, w_down):
    rows, d = xs.shape
    d_ff = w_down.shape[1]
    return pl.pallas_call(
        functools.partial(_ffn_kernel, tm, seg_len // tm, d_ff),
        grid=(rows // tm,),
        in_specs=_row_specs(rows, d, tm, HALO_F) + [
            _mod_spec(layer, d, mod_row), _layer_spec(layer, (1, d)), _layer_spec(layer, (1, d)),
            _layer_spec(layer, (d, 2 * d_ff)), _layer_spec(layer, (FFN_CONV, 2 * d_ff)),
            _layer_spec(layer, (1, 2 * d_ff)), _layer_spec(layer, (d_ff, d))],
        out_specs=pl.BlockSpec((tm, d), lambda t: (t, 0)),
        out_shape=jax.ShapeDtypeStruct((rows, d), F32),
        scratch_shapes=[pltpu.VMEM((tm + 2 * HALO_F, FFN_COL_CHUNK), F32)] * 2,
        compiler_params=_params(1),
        name="conv_ffn",
    )(xs, xs, xs, mod, gpre, gpost, w_up, conv_w, conv_b, w_down)


def _rope_tables(seq):
    rows = seq // GRID_W
    row = jnp.repeat(jnp.arange(rows, dtype=F32), GRID_W)
    col = jnp.tile(jnp.arange(GRID_W, dtype=F32), rows)
    quarter = DA_HEAD_DIM // 4
    inv_freq = ROPE_BASE ** (-jnp.arange(quarter, dtype=F32) / quarter)
    ang = jnp.concatenate([row[:, None] * inv_freq, col[:, None] * inv_freq], axis=-1)
    cos, sin = jnp.cos(ang), jnp.sin(ang)
    return jnp.concatenate([cos] * 4, axis=-1), jnp.concatenate([-sin, sin, -sin, sin], axis=-1)


def kernel(x, c, ctx, c_ctx, w_mod, b_mod, mix_norm_pre, mix_norm_post, w_in, da_lambda, da_subln, ssd_conv_w, ssd_conv_b, ssd_a_log, ssd_dt_bias, ssd_d, ssd_norm, gqa_q_norm, gqa_k_norm, conf_conv_w, conf_conv_b, conf_ln_g, conf_ln_b, w_branch, w_out, ffn_norm_pre, ffn_norm_post, ffn_w_up, ffn_conv_w, ffn_conv_b, ffn_w_down):
    n_batch, seq, d = x.shape
    n_ctx = ctx.shape[1]
    depth = w_mod.shape[0]
    tm_l = min(TM_LATENT, seq)
    tq = min(TQ, seq)
    assert n_ctx == CHUNK and seq % tm_l == 0 and seq % tq == 0 and seq % GRID_W == 0 and n_batch < 8

    o_dt = C_XBC + 1024
    o_gate = w_in.shape[-1] - N_BRANCH * d
    w_a = jnp.concatenate([w_in[:, :, :o_dt], w_in[:, :, o_dt + 2 * SSD_HEADS:o_gate],
                           w_in[:, :, o_dt:o_dt + 2 * SSD_HEADS],
                           jnp.zeros((depth, d, C_END - C_DT - 2 * SSD_HEADS), F32)], axis=-1).astype(BF16)
    w_gate = w_in[:, :, o_gate:].astype(BF16)
    w_br = w_branch.astype(BF16)
    w_o = w_out.astype(BF16)
    w_up = ffn_w_up.astype(BF16)
    w_dn = ffn_w_down.astype(BF16)

    def row3(p):
        return p.reshape(depth, 1, -1)

    def pad_lanes(p):
        flat = p.reshape(depth, 1, -1)
        return jnp.pad(flat, ((0, 0), (0, 0), (0, LANES - flat.shape[-1])))

    qn = row3(jnp.concatenate([gqa_q_norm, gqa_q_norm], axis=-1))
    kn = row3(jnp.concatenate([gqa_k_norm, gqa_k_norm], axis=-1))
    skip = jnp.repeat(ssd_d, SSD_HEAD_DIM, axis=-1).reshape(depth, 2, 1, SSD_HEADS * SSD_HEAD_DIM)
    rope = _rope_tables(seq)

    cvec = jnp.concatenate([c, c_ctx[None, :], jnp.zeros((8 - n_batch - 1, d), F32)], axis=0)
    mod = _modulation(cvec, w_mod, b_mod).reshape(depth, 8, 6, d)
    tiles_l = seq // tm_l

    def row_latent(t):
        return t // tiles_l

    def row_ctx(t):
        return n_batch

    xl = x.reshape(n_batch * seq, d)
    xc = ctx.reshape(n_batch * n_ctx, d)
    for l in range(depth):
        need_ctx = l < depth - 1
        lambda_init = 0.8 - 0.6 * math.exp(-0.3 * l)
        proj = (row3(mix_norm_pre), w_a, qn, kn, ssd_conv_w, row3(ssd_conv_b), conf_conv_w, row3(conf_conv_b),
                row3(conf_ln_g), row3(conf_ln_b))
        qa, kat, va, z, sx, dt, gq, gkt, gv, cf = _inproj(xl, seq, tm_l, l, mod, row_latent, rope, *proj)
        qa_c, kat_c, va_c, z_c, sx_c, dt_c, gq_c, gkt_c, gv_c, cf_c = _inproj(
            xc, n_ctx, n_ctx, l, mod, row_ctx, None, *proj)
        oa = _diff_attention(qa, tq, [(kat, va), (kat_c, va_c)], da_lambda, row3(da_subln), l, lambda_init)
        og = _gqa_attention(gq, tq, [(gkt, gv), (gkt_c, gv_c)])
        yf, yb, yf_c, yb_c = _ssd(sx, dt, sx_c, dt_c, pad_lanes(ssd_a_log), pad_lanes(ssd_dt_bias), skip, l, n_batch)
        mix = (row3(mix_norm_pre), row3(mix_norm_post))
        mixw = (w_gate, w_br, w_o)
        ffn = (row3(ffn_norm_pre), row3(ffn_norm_post), w_up, ffn_conv_w, row3(ffn_conv_b), w_dn)
        xl = _merge(xl, tm_l, l, mod, row_latent, *mix, oa, yf, yb, z, row3(ssd_norm), og, cf, *mixw)
        xl = _ffn(xl, seq, tm_l, l, mod, row_latent, *ffn)
        if need_ctx:
            oa_c = _diff_attention(qa_c, n_ctx, [(kat_c, va_c)], da_lambda, row3(da_subln), l, lambda_init)
            og_c = _gqa_attention(gq_c, n_ctx, [(gkt_c, gv_c)])
            xc = _merge(xc, n_ctx, l, mod, row_ctx, *mix, oa_c, yf_c, yb_c, z_c, row3(ssd_norm), og_c, cf_c, *mixw)
            xc = _ffn(xc, n_ctx, n_ctx, l, mod, row_ctx, *ffn)
    return xl.reshape(n_batch, seq, d)
```

```python
import functools
import math

import jax
import jax.numpy as jnp
from jax import lax
from jax.experimental import pallas as pl
from jax.experimental.pallas import tpu as pltpu

F32 = jnp.float32
BF16 = jnp.bfloat16

GRID_W = 64
ROPE_BASE = 10000.0
EPS = 1e-6
N_BRANCH = 4
BRANCH_W = 512
DA_HEADS = 4
DA_HEAD_DIM = 64
SSD_HEADS = 8
SSD_HEAD_DIM = 64
SSD_GROUPS = 2
SSD_HPG = SSD_HEADS // SSD_GROUPS
SSD_STATE = 128
SSD_CONV = 5
GQA_HEADS = 8
GQA_KV_HEADS = 2
GQA_HPG = GQA_HEADS // GQA_KV_HEADS
GQA_HEAD_DIM = 64
CONF_CONV = 31
FFN_CONV = 3

LANES = 128
SUBLANES = 8
TM_LATENT = 512
CHUNK = 256
HALO_A = 16
HALO_F = 8
TQ = 1024
KEY_CHUNK = 1024
DA_SUB = 512
GQA_SUB = 256
VMEM_LIMIT = 56 * 1024 * 1024

C_DAQ, C_DAK, C_DAV, C_Z, C_XBC, C_DT, C_GQ, C_GK, C_GV, C_CONF, C_GATE, C_END = (
    0, 512, 1024, 1536, 2048, 3072, 3200, 3712, 3840, 3968, 5120, 9216)
LOG2E = math.log2(math.e)


def _params(n_axes):
    return pltpu.CompilerParams(dimension_semantics=("arbitrary",) * n_axes, vmem_limit_bytes=VMEM_LIMIT)


def _resident(shape, index_map):
    return pl.BlockSpec(shape, index_map, pipeline_mode=pl.Buffered(1))


def _sigmoid(v):
    return 1.0 / (1.0 + jnp.exp(-v))


def _silu(v):
    return v * _sigmoid(v)


def _rms(v, gain):
    return v * lax.rsqrt(jnp.mean(v * v, axis=-1, keepdims=True) + EPS) * gain


def _dot(a, b):
    return jnp.dot(a, b, preferred_element_type=F32)


def _dot_nt(a, b):
    return lax.dot_general(a, b, (((1,), (1,)), ((), ())), preferred_element_type=F32)


def _split2(v):
    hi = v.astype(BF16)
    lo = (v - hi.astype(F32)).astype(BF16)
    return hi, lo


def _mod_kernel(c_ref, w_ref, b_ref, o_ref):
    s = _silu(c_ref[...])
    hi, lo = _split2(s)
    lhs = jnp.concatenate([hi, lo], axis=0)
    r = _dot(lhs, w_ref[...].astype(BF16))
    o_ref[...] = r[0:8] + r[8:16] + b_ref[...]


def _modulation(cvec, w_mod, b_mod):
    depth, d, n = w_mod.shape
    tn = 1536
    return pl.pallas_call(
        _mod_kernel,
        grid=(depth, n // tn),
        in_specs=[pl.BlockSpec((8, d), lambda l, j: (0, 0)),
                  pl.BlockSpec((None, d, tn), lambda l, j: (l, 0, j)),
                  pl.BlockSpec((None, 1, tn), lambda l, j: (l, 0, j))],
        out_specs=pl.BlockSpec((None, 8, tn), lambda l, j: (l, 0, j)),
        out_shape=jax.ShapeDtypeStruct((depth, 8, n), F32),
        compiler_params=_params(2),
        name="modulation",
    )(cvec, w_mod, b_mod.reshape(depth, 1, n))


def _row_specs(rows, d, tm, halo):
    hb = tm // halo
    last = rows // halo - 1
    return [pl.BlockSpec((tm, d), lambda t: (t, 0)),
            pl.BlockSpec((halo, d), lambda t: (jnp.maximum(t * hb - 1, 0), 0)),
            pl.BlockSpec((halo, d), lambda t: (jnp.minimum((t + 1) * hb, last), 0))]


def _mod_spec(layer, d, mod_row):
    return pl.BlockSpec((None, None, 6, d), lambda t: (layer, mod_row(t), 0, 0))


def _layer_spec(layer, shape):
    return _resident((None,) + shape, lambda t: (layer,) + (0,) * len(shape))


def _rope128(tb, cos, sin, low32):
    up = pltpu.roll(tb, LANES - 32, axis=1)
    dn = pltpu.roll(tb, 32, axis=1)
    return tb * cos + jnp.where(low32, up, dn) * sin


def _inproj_kernel(tm, tiles_per_seg, use_rope, *refs):
    if use_rope:
        (xm, xp, xn, mod, gpre, w, qn, kn, scw, scb, ccw, ccb, lng, lnb, cos_r, sin_r,
         qa_o, kat_o, va_o, z_o, xc_o, dt_o, gq_o, gkt_o, gv_o, cf_o, sbuf, rbuf) = refs
    else:
        (xm, xp, xn, mod, gpre, w, qn, kn, scw, scb, ccw, ccb, lng, lnb,
         qa_o, kat_o, va_o, z_o, xc_o, dt_o, gq_o, gkt_o, gv_o, cf_o, sbuf, rbuf) = refs
    ti = pl.program_id(0) % tiles_per_seg
    prev_ok = ti > 0
    next_ok = ti < tiles_per_seg - 1
    shift, scale, gain = mod[0:1, :], mod[1:2, :], gpre[...]

    def normmod(v):
        return _rms(v, gain) * (1.0 + scale) + shift

    hm = normmod(xm[...]).astype(BF16)
    hp = jnp.where(prev_ok, normmod(xp[...]), 0.0).astype(BF16)
    hn = jnp.where(next_ok, normmod(xn[...]), 0.0).astype(BF16)
    hext = jnp.concatenate([hp, hm, hn], axis=0)

    lane = lax.broadcasted_iota(jnp.int32, (tm, LANES), 1)
    low32 = (lane % 64) < 32
    gi = lax.broadcasted_iota(jnp.int32, (LANES, LANES), 0) // 64
    gj = lax.broadcasted_iota(jnp.int32, (LANES, LANES), 1) // 64
    gsum = (gi == gj).astype(BF16)

    def rope(tb):
        return _rope128(tb, cos_r[...], sin_r[...], low32) if use_rope else tb

    def headnorm(tb, g):
        ss = _dot((tb * tb).astype(BF16), gsum)
        return tb * lax.rsqrt(ss * (1.0 / GQA_HEAD_DIM) + EPS) * g

    def halves(c0):
        r = _dot(hm, w[:, c0:c0 + 2 * LANES])
        return r[:, 0:LANES], r[:, LANES:2 * LANES]

    a = _dot(hext, w[:, C_CONF:C_CONF + BRANCH_W])
    g = _dot(hext, w[:, C_CONF + BRANCH_W:C_CONF + 2 * BRANCH_W])
    sbuf[...] = _dot(hext, w[:, C_XBC:C_DT])
    rbuf[0] = a * _sigmoid(g)
    span = tm + 2 * HALO_A - SUBLANES
    for r in range(1, SUBLANES):
        rbuf[r, 0:span, :] = rbuf[0, pl.ds(r, span), :]
    acc = ccb[...]
    for k_tap in range(CONF_CONV):
        off = HALO_A - CONF_CONV // 2 + k_tap
        acc = acc + ccw[k_tap:k_tap + 1, :] * rbuf[off % SUBLANES, pl.ds(off - off % SUBLANES, tm), :]
    xc = acc - jnp.mean(acc, axis=-1, keepdims=True)
    y = xc * lax.rsqrt(jnp.mean(xc * xc, axis=-1, keepdims=True) + EPS) * lng[...] + lnb[...]
    cf_o[...] = _silu(y).astype(BF16)

    acc = scb[...] + scw[0:1, :] * sbuf[pl.ds(HALO_A - SSD_CONV // 2, tm), :]
    for k_tap in range(1, SSD_CONV):
        acc = acc + scw[k_tap:k_tap + 1, :] * sbuf[pl.ds(HALO_A - SSD_CONV // 2 + k_tap, tm), :]
    xc_o[...] = _silu(acc)

    for j in range(2):
        for i, q in enumerate(halves(C_DAQ + j * 2 * LANES)):
            blk = 2 * j + i
            qa_o[:, blk * LANES:(blk + 1) * LANES] = (rope(q) * (DA_HEAD_DIM ** -0.5 * LOG2E)).astype(BF16)
        for i, k in enumerate(halves(C_DAK + j * 2 * LANES)):
            blk = 2 * j + i
            kat_o[blk * LANES:(blk + 1) * LANES, :] = rope(k).T.astype(BF16)
    va_o[...] = _dot(hm, w[:, C_DAV:C_Z]).astype(BF16)
    z_o[...] = _dot(hm, w[:, C_Z:C_XBC])
    dt_o[...] = _dot(hm, w[:, C_DT:C_GQ])

    for j in range(2):
        for i, q in enumerate(halves(C_GQ + j * 2 * LANES)):
            blk = 2 * j + i
            q = rope(headnorm(q, qn[...]))
            gq_o[:, blk * LANES:(blk + 1) * LANES] = (q * (GQA_HEAD_DIM ** -0.5 * LOG2E)).astype(BF16)
    k, v = halves(C_GK)
    gkt_o[...] = rope(headnorm(k, kn[...])).T.astype(BF16)
    gv_o[...] = v.astype(BF16)


def _inproj(xs, seg_len, tm, layer, mod, mod_row, rope, gpre, w_a, qn, kn, scw, scb, ccw, ccb, lng, lnb):
    rows, d = xs.shape
    tps = seg_len // tm
    n_seg = rows // seg_len

    def out(width, dtype):
        return pl.BlockSpec((tm, width), lambda t: (t, 0)), jax.ShapeDtypeStruct((rows, width), dtype)

    def out_t(width):
        return (pl.BlockSpec((None, width, tm), lambda t: (t // tps, 0, t % tps)),
                jax.ShapeDtypeStruct((n_seg, width, seg_len), BF16))

    outs = [out(512, BF16), out_t(512), out(512, BF16), out(512, F32), out(1024, F32), out(LANES, F32),
            out(512, BF16), out_t(LANES), out(LANES, BF16), out(512, BF16)]
    in_specs = _row_specs(rows, d, tm, HALO_A) + [
        _mod_spec(layer, d, mod_row), _layer_spec(layer, (1, d)), _layer_spec(layer, (d, C_GATE)),
        _layer_spec(layer, (1, LANES)), _layer_spec(layer, (1, LANES)),
        _layer_spec(layer, (SSD_CONV, 1024)), _layer_spec(layer, (1, 1024)),
        _layer_spec(layer, (CONF_CONV, BRANCH_W)), _layer_spec(layer, (1, BRANCH_W)),
        _layer_spec(layer, (1, BRANCH_W)), _layer_spec(layer, (1, BRANCH_W))]
    args = [xs, xs, xs, mod, gpre, w_a, qn, kn, scw, scb, ccw, ccb, lng, lnb]
    if rope is not None:
        in_specs += [pl.BlockSpec((tm, LANES), lambda t: (t % tps, 0))] * 2
        args += list(rope)
    return pl.pallas_call(
        functools.partial(_inproj_kernel, tm, tps, rope is not None),
        grid=(rows // tm,),
        in_specs=in_specs,
        out_specs=[o[0] for o in outs],
        out_shape=[o[1] for o in outs],
        scratch_shapes=[pltpu.VMEM((tm + 2 * HALO_A, 1024), F32),
                        pltpu.VMEM((SUBLANES, tm + 2 * HALO_A, BRANCH_W), F32)],
        compiler_params=_params(1),
        name="inproj",
    )(*args)


def _key_chunks(seg_lens):
    return [(s, off, min(KEY_CHUNK, n - off)) for s, n in enumerate(seg_lens) for off in range(0, n, KEY_CHUNK)]


def _online_softmax_pv(score_fn, value_fn, chunks):
    m, acc = None, None
    s_next = score_fn(chunks[0])
    for i, c in enumerate(chunks):
        s = s_next
        if i + 1 < len(chunks):
            s_next = score_fn(chunks[i + 1])
        mc = jnp.max(s, axis=-1, keepdims=True)
        if m is None:
            m = mc
            acc = _dot(jnp.exp2(s - m).astype(BF16), value_fn(c))
        else:
            m_new = jnp.maximum(m, mc)
            acc = acc * jnp.exp2(m - m_new) + _dot(jnp.exp2(s - m_new).astype(BF16), value_fn(c))
            m = m_new
    return acc


def _da_kernel(lambda_init, sub, seg_lens, q_ref, *refs):
    n_seg = len(seg_lens)
    kt_refs, v_refs = refs[0:2 * n_seg:2], refs[1:2 * n_seg:2]
    lam_ref, sub_ref, o_ref = refs[2 * n_seg:]
    chunks = _key_chunks(seg_lens)
    lp = lam_ref[...]
    lam = (jnp.exp(jnp.sum(lp[0:1] * lp[1:2], axis=-1, keepdims=True))
           - jnp.exp(jnp.sum(lp[2:3] * lp[3:4], axis=-1, keepdims=True)) + lambda_init)
    lane = lax.broadcasted_iota(jnp.int32, (sub, LANES), 1)

    def tile(i, carry):
        rs = pl.ds(pl.multiple_of(i * sub, sub), sub)
        for h in range(DA_HEADS):
            blk, half = h // 2, h % 2
            keep = (lane // 64) == half
            q1 = q_ref[rs, blk * LANES:(blk + 1) * LANES]
            q2 = q_ref[rs, (2 + blk) * LANES:(3 + blk) * LANES]
            q1 = jnp.where(keep, q1, jnp.zeros_like(q1))
            q2 = jnp.where(keep, q2, jnp.zeros_like(q2))

            def scores(c):
                s, off, n = c
                s1 = _dot(q1, kt_refs[s][blk * LANES:(blk + 1) * LANES, off:off + n])
                s2 = _dot(q2, kt_refs[s][(2 + blk) * LANES:(3 + blk) * LANES, off:off + n])
                return jnp.concatenate([s1, s2], axis=0)

            def values(c):
                s, off, n = c
                v = v_refs[s][off:off + n, h * LANES:(h + 1) * LANES]
                return jnp.concatenate([v, jnp.ones_like(v)], axis=1)

            acc = _online_softmax_pv(scores, values, chunks)
            o1 = acc[0:sub, 0:LANES] / acc[0:sub, LANES:2 * LANES]
            o2 = acc[sub:2 * sub, 0:LANES] / acc[sub:2 * sub, LANES:2 * LANES]
            o = _rms(o1 - lam * o2, sub_ref[...]) * (1.0 - lambda_init)
            o_ref[rs, h * LANES:(h + 1) * LANES] = o.astype(BF16)
        return carry

    lax.fori_loop(0, q_ref.shape[0] // sub, tile, 0)


def _diff_attention(q, tq, segs, da_lambda, subln, layer, lambda_init):
    rows = q.shape[0]
    n_batch = segs[0][0].shape[0]
    nq = rows // n_batch // tq
    seg_lens = tuple(kt.shape[2] for kt, _ in segs)
    in_specs = [pl.BlockSpec((tq, 512), lambda b, i: (b * nq + i, 0))]
    args = [q]
    for kt, v in segs:
        n = kt.shape[2]
        in_specs += [pl.BlockSpec((None, 512, n), lambda b, i: (b, 0, 0)),
                     pl.BlockSpec((None, n, 512), lambda b, i: (b, 0, 0))]
        args += [kt, v.reshape(n_batch, n, 512)]
    in_specs += [pl.BlockSpec((None, 4, DA_HEAD_DIM), lambda b, i: (layer, 0, 0)),
                 pl.BlockSpec((None, 1, LANES), lambda b, i: (layer, 0, 0))]
    return pl.pallas_call(
        functools.partial(_da_kernel, lambda_init, min(DA_SUB, tq), seg_lens),
        grid=(n_batch, nq),
        in_specs=in_specs,
        out_specs=pl.BlockSpec((tq, 512), lambda b, i: (b * nq + i, 0)),
        out_shape=jax.ShapeDtypeStruct((rows, 512), BF16),
        compiler_params=_params(2),
        name="diff_attention",
    )(*args, da_lambda, subln)


def _gqa_kernel(sub, seg_lens, q_ref, *refs):
    n_seg = len(seg_lens)
    kt_refs, v_refs = refs[0:2 * n_seg:2], refs[1:2 * n_seg:2]
    o_ref = refs[2 * n_seg]
    chunks = _key_chunks(seg_lens)
    lane = lax.broadcasted_iota(jnp.int32, (sub, LANES), 1)
    lane_st = lax.broadcasted_iota(jnp.int32, (GQA_HPG * sub, LANES), 1)

    def tile(i, carry):
        rs = pl.ds(pl.multiple_of(i * sub, sub), sub)
        for g in range(GQA_KV_HEADS):
            in_group = (lane // 64) == g
            stack = []
            for r in range(GQA_HPG):
                head = g * GQA_HPG + r
                qb = q_ref[rs, (head // 2) * LANES:(head // 2 + 1) * LANES].astype(F32)
                if head % 2 != g:
                    qb = pltpu.roll(qb, 64, axis=1)
                stack.append(jnp.where(in_group, qb, 0.0).astype(BF16))
            qst = jnp.concatenate(stack, axis=0)

            def scores(c):
                s, off, n = c
                return _dot(qst, kt_refs[s][:, off:off + n])

            def values(c):
                s, off, n = c
                v = v_refs[s][off:off + n, :]
                vl = lax.broadcasted_iota(jnp.int32, v.shape, 1)
                return jnp.where((vl // 64) == g, v, jnp.ones_like(v))

            acc = _online_softmax_pv(scores, values, chunks)
            den = jnp.where((lane_st // 64) == g, pltpu.roll(acc, 64, axis=1), 1.0)
            o = acc / den
            for pair in range(GQA_HPG // 2):
                pieces = []
                for half in range(2):
                    r = 2 * pair + half
                    piece = o[r * sub:(r + 1) * sub]
                    pieces.append(piece if half == g else pltpu.roll(piece, 64, axis=1))
                blk = (g * GQA_HPG) // 2 + pair
                o_ref[rs, blk * LANES:(blk + 1) * LANES] = jnp.where(lane < 64, pieces[0], pieces[1]).astype(BF16)
        return carry

    lax.fori_loop(0, q_ref.shape[0] // sub, tile, 0)


def _gqa_attention(q, tq, segs):
    rows = q.shape[0]
    n_batch = segs[0][0].shape[0]
    nq = rows // n_batch // tq
    seg_lens = tuple(kt.shape[2] for kt, _ in segs)
    in_specs = [pl.BlockSpec((tq, 512), lambda b, i: (b * nq + i, 0))]
    args = [q]
    for kt, v in segs:
        n = kt.shape[2]
        in_specs += [pl.BlockSpec((None, LANES, n), lambda b, i: (b, 0, 0)),
                     pl.BlockSpec((None, n, LANES), lambda b, i: (b, 0, 0))]
        args += [kt, v.reshape(n_batch, n, LANES)]
    return pl.pallas_call(
        functools.partial(_gqa_kernel, min(GQA_SUB, tq), seg_lens),
        grid=(n_batch, nq),
        in_specs=in_specs,
        out_specs=pl.BlockSpec((tq, 512), lambda b, i: (b * nq + i, 0)),
        out_shape=jax.ShapeDtypeStruct((rows, 512), BF16),
        compiler_params=_params(2),
        name="gqa_attention",
    )(*args)


def _ssd_direction(d, x_ref, dt_ref, alog_ref, bias_ref, skip_ref, y_ref, h_ref):
    q = CHUNK
    lane = lax.broadcasted_iota(jnp.int32, (1, LANES), 1)
    a_vec = jnp.where(lane < 2 * SSD_HEADS, -jnp.exp(alog_ref[...]), 0.0)
    raw = dt_ref[...] + bias_ref[...]
    dt = jnp.maximum(raw, 0.0) + jnp.log1p(jnp.exp(-jnp.abs(raw)))
    da = dt * a_vec
    ri = lax.broadcasted_iota(jnp.int32, (q, q), 0)
    ci = lax.broadcasted_iota(jnp.int32, (q, q), 1)
    tri = (ci <= ri) if d == 0 else (ci >= ri)
    hi = da.astype(BF16).astype(F32)
    rest = da - hi
    mid = rest.astype(BF16).astype(F32)
    packed = hi + pltpu.roll(mid, 2 * SSD_HEADS, axis=1) + pltpu.roll(rest - mid, 4 * SSD_HEADS, axis=1)
    part = _dot(tri.astype(BF16), packed.astype(BF16))
    cs = part + pltpu.roll(part, LANES - 2 * SSD_HEADS, axis=1) + pltpu.roll(part, LANES - 4 * SSD_HEADS, axis=1)
    cs = jnp.where(lane < 2 * SSD_HEADS, cs, 0.0)
    cs_t = cs.T
    tot = jnp.sum(da, axis=0, keepdims=True)

    er = lax.broadcasted_iota(jnp.int32, (2 * LANES, 512), 0) % LANES
    ec = lax.broadcasted_iota(jnp.int32, (2 * LANES, 512), 1) // SSD_HEAD_DIM
    expand = (er == d * SSD_HEADS + ec).astype(BF16)
    stack = jnp.concatenate([dt, jnp.exp(tot - cs), jnp.exp(cs), jnp.broadcast_to(jnp.exp(tot), (8, LANES))], axis=0)
    hi, lo = _split2(stack)
    wide = _dot(jnp.concatenate([hi, lo], axis=1), expand)
    dt_e, dte_e, ecs_e, cdec_e = wide[0:q], wide[q:2 * q], wide[2 * q:3 * q], wide[3 * q:3 * q + 1]

    xs = x_ref[:, 0:512]
    xsdt = xs * dt_e
    lane_g = lax.broadcasted_iota(jnp.int32, (q, 2 * LANES), 1) // SSD_HEAD_DIM
    for g in range(SSD_GROUPS):
        gl = slice(g * 2 * LANES, (g + 1) * 2 * LANES)
        bm = x_ref[:, 512 + g * LANES:512 + (g + 1) * LANES]
        cm = x_ref[:, 768 + g * LANES:768 + (g + 1) * LANES].astype(BF16)
        cb = _dot_nt(cm, bm.astype(BF16))
        xg = xsdt[:, gl]
        y = _dot(cm, h_ref[g].astype(BF16)) * ecs_e[:, gl]
        for r in range(SSD_HPG):
            idx = d * SSD_HEADS + g * SSD_HPG + r
            seg = cs[:, idx:idx + 1] - cs_t[idx:idx + 1, :]
            lmat = jnp.exp(jnp.where(tri, seg, -jnp.inf))
            y = y + _dot((cb * lmat).astype(BF16), jnp.where(lane_g == r, xg, 0.0).astype(BF16))
        y_ref[:, gl] = y + skip_ref[:, gl] * xs[:, gl]
        h_ref[g] = cdec_e[:, gl] * h_ref[g] + _dot(bm.T.astype(BF16), (xg * dte_e[:, gl]).astype(BF16))


def _ssd_kernel(xf, dtf, xb, dtb, xc, dtc, alog, bias, skip, yf, yb, yfc, ybc, h_ref):
    t = pl.program_id(1)

    @pl.when(t == 0)
    def _():
        h_ref[...] = jnp.zeros_like(h_ref)
        yf[...] = jnp.zeros_like(yf)
        yb[...] = jnp.zeros_like(yb)
        _ssd_direction(0, xc, dtc, alog, bias, skip.at[0], yfc, h_ref.at[0])
        _ssd_direction(1, xc, dtc, alog, bias, skip.at[1], ybc, h_ref.at[1])

    @pl.when(t != 0)
    def _():
        _ssd_direction(0, xf, dtf, alog, bias, skip.at[0], yf, h_ref.at[0])
        _ssd_direction(1, xb, dtb, alog, bias, skip.at[1], yb, h_ref.at[1])


def _ssd(xc_l, dt_l, xc_c, dt_c, alog, bias, skip, layer, n_batch):
    rows_l, rows_c = xc_l.shape[0], xc_c.shape[0]
    n_l = rows_l // n_batch // CHUNK

    def fwd(b, t):
        return (b * n_l + jnp.maximum(t - 1, 0), 0)

    def bwd(b, t):
        return (b * n_l + n_l - jnp.maximum(t, 1), 0)

    def cxt(b, t):
        return (b, 0)

    def lyr(shape):
        return pl.BlockSpec((None,) + shape, lambda b, t: (layer,) + (0,) * len(shape))

    return pl.pallas_call(
        _ssd_kernel,
        grid=(n_batch, n_l + 1),
        in_specs=[pl.BlockSpec((CHUNK, 1024), fwd), pl.BlockSpec((CHUNK, LANES), fwd),
                  pl.BlockSpec((CHUNK, 1024), bwd), pl.BlockSpec((CHUNK, LANES), bwd),
                  pl.BlockSpec((CHUNK, 1024), cxt), pl.BlockSpec((CHUNK, LANES), cxt),
                  lyr((1, LANES)), lyr((1, LANES)), lyr((2, 1, 512))],
        out_specs=[pl.BlockSpec((CHUNK, 512), fwd), pl.BlockSpec((CHUNK, 512), bwd),
                   pl.BlockSpec((CHUNK, 512), cxt), pl.BlockSpec((CHUNK, 512), cxt)],
        out_shape=[jax.ShapeDtypeStruct((rows_l, 512), F32)] * 2 + [jax.ShapeDtypeStruct((rows_c, 512), F32)] * 2,
        scratch_shapes=[pltpu.VMEM((2, SSD_GROUPS, SSD_STATE, 2 * LANES), F32)],
        compiler_params=_params(2),
        name="ssd_scan",
    )(xc_l, dt_l, xc_l, dt_l, xc_c, dt_c, alog, bias, skip)


def _merge_kernel(x_ref, mod, gpre, gpost, oa, yf, yb, z, ssdn, og, cf, wg0, wg1, wg2, wg3, wb, wo, o_ref):
    x = x_ref[...]
    h = (_rms(x, gpre[...]) * (1.0 + mod[1:2, :]) + mod[0:1, :]).astype(BF16)
    ys = _rms((yf[...] + yb[...]) * _silu(z[...]), ssdn[...]).astype(BF16)
    branches = [oa[...], ys, og[...], cf[...]]
    gates = [wg0, wg1, wg2, wg3]
    mixed = None
    for k in range(N_BRANCH):
        term = _sigmoid(_dot(h, gates[k][...])) * _dot(branches[k], wb[k])
        mixed = term if mixed is None else mixed + term
    out = _dot(mixed.astype(BF16), wo[...])
    o_ref[...] = x + mod[2:3, :] * _rms(out, gpost[...])


def _merge(xs, tm, layer, mod, mod_row, gpre, gpost, oa, yf, yb, z, ssdn, og, cf, w_all, w_branch, w_out):
    rows, d = xs.shape

    def tile(width):
        return pl.BlockSpec((tm, width), lambda t: (t, 0))

    return pl.pallas_call(
        _merge_kernel,
        grid=(rows // tm,),
        in_specs=[tile(d), _mod_spec(layer, d, mod_row), _layer_spec(layer, (1, d)), _layer_spec(layer, (1, d)),
                  tile(512), tile(512), tile(512), tile(512), _layer_spec(layer, (1, BRANCH_W)), tile(512), tile(512),
                  *[_resident((None, d, d), lambda t, k=k: (layer, 0, C_GATE // d + k)) for k in range(N_BRANCH)],
                  _layer_spec(layer, (N_BRANCH, BRANCH_W, d)),
                  _layer_spec(layer, (d, d))],
        out_specs=tile(d),
        out_shape=jax.ShapeDtypeStruct((rows, d), F32),
        compiler_params=_params(1),
        name="merge",
    )(xs, mod, gpre, gpost, oa, yf, yb, z, ssdn, og, cf, w_all, w_all, w_all, w_all, w_branch, w_out)


FFN_COL_CHUNK = 512


def _ffn_chunks(d_ff):
    return [(c0, min(FFN_COL_CHUNK, d_ff - c0)) for c0 in range(0, d_ff, FFN_COL_CHUNK)]


def _ffn_kernel(tm, tiles_per_seg, d_ff, xm, xp, xn, mod, gpre, gpost, wu, cw_ref, cb_ref, wd, o_ref, abuf, gbuf):
    ti = pl.program_id(0) % tiles_per_seg
    prev_ok = ti > 0
    next_ok = ti < tiles_per_seg - 1
    shift, scale, gain = mod[3:4, :], mod[4:5, :], gpre[...]

    def normmod(v):
        return _rms(v, gain) * (1.0 + scale) + shift

    x = xm[...]
    hext = jnp.concatenate([jnp.where(prev_ok, normmod(xp[...]), 0.0), normmod(x),
                            jnp.where(next_ok, normmod(xn[...]), 0.0)], axis=0).astype(BF16)

    def conv(buf, c0, cw):
        acc = cb_ref[:, c0:c0 + cw] + cw_ref[0:1, c0:c0 + cw] * buf[pl.ds(HALO_F - 1, tm), 0:cw]
        for k in range(1, FFN_CONV):
            acc = acc + cw_ref[k:k + 1, c0:c0 + cw] * buf[pl.ds(HALO_F - 1 + k, tm), 0:cw]
        return acc

    chunks = _ffn_chunks(d_ff)

    def up(i):
        c0, cw = chunks[i]
        abuf[i % 2, :, 0:cw] = _dot(hext, wu[:, c0:c0 + cw])
        gbuf[i % 2, :, 0:cw] = _dot(hext, wu[:, d_ff + c0:d_ff + c0 + cw])

    up(0)
    out = None
    for i, (c0, cw) in enumerate(chunks):
        if i + 1 < len(chunks):
            up(i + 1)
        act = (_silu(conv(gbuf.at[i % 2], d_ff + c0, cw)) * conv(abuf.at[i % 2], c0, cw)).astype(BF16)
        part = _dot(act, wd[c0:c0 + cw, :])
        out = part if out is None else out + part
    o_ref[...] = x + mod[5:6, :] * _rms(out, gpost[...])


def _ffn(xs, seg_len, tm, layer, mod, mod_row, gpre, gpost, w_up, conv_w, conv_b, w_down):
    rows, d = xs.shape
    d_ff = w_down.shape[1]
    return pl.pallas_call(
        functools.partial(_ffn_kernel, tm, seg_len // tm, d_ff),
        grid=(rows // tm,),
        in_specs=_row_specs(rows, d, tm, HALO_F) + [
            _mod_spec(layer, d, mod_row), _layer_spec(layer, (1, d)), _layer_spec(layer, (1, d)),
            _layer_spec(layer, (d, 2 * d_ff)), _layer_spec(layer, (FFN_CONV, 2 * d_ff)),
            _layer_spec(layer, (1, 2 * d_ff)), _layer_spec(layer, (d_ff, d))],
        out_specs=pl.BlockSpec((tm, d), lambda t: (t, 0)),
        out_shape=jax.ShapeDtypeStruct((rows, d), F32),
        scratch_shapes=[pltpu.VMEM((2, tm + 2 * HALO_F, FFN_COL_CHUNK), F32)] * 2,
        compiler_params=_params(1),
        name="conv_ffn",
    )(xs, xs, xs, mod, gpre, gpost, w_up, conv_w, conv_b, w_down)


def _rope_tables(seq):
    rows = seq // GRID_W
    row = jnp.repeat(jnp.arange(rows, dtype=F32), GRID_W)
    col = jnp.tile(jnp.arange(GRID_W, dtype=F32), rows)
    quarter = DA_HEAD_DIM // 4
    inv_freq = ROPE_BASE ** (-jnp.arange(quarter, dtype=F32) / quarter)
    ang = jnp.concatenate([row[:, None] * inv_freq, col[:, None] * inv_freq], axis=-1)
    cos, sin = jnp.cos(ang), jnp.sin(ang)
    return jnp.concatenate([cos] * 4, axis=-1), jnp.concatenate([-sin, sin, -sin, sin], axis=-1)


def kernel(x, c, ctx, c_ctx, w_mod, b_mod, mix_norm_pre, mix_norm_post, w_in, da_lambda, da_subln, ssd_conv_w, ssd_conv_b, ssd_a_log, ssd_dt_bias, ssd_d, ssd_norm, gqa_q_norm, gqa_k_norm, conf_conv_w, conf_conv_b, conf_ln_g, conf_ln_b, w_branch, w_out, ffn_norm_pre, ffn_norm_post, ffn_w_up, ffn_conv_w, ffn_conv_b, ffn_w_down):
    n_batch, seq, d = x.shape
    n_ctx = ctx.shape[1]
    depth = w_mod.shape[0]
    tm_l = min(TM_LATENT, seq)
    tq = min(TQ, seq)
    assert n_ctx == CHUNK and seq % tm_l == 0 and seq % tq == 0 and seq % GRID_W == 0 and n_batch < 8

    o_gq = C_DT + 2 * SSD_HEADS
    o_gate = w_in.shape[-1] - N_BRANCH * d
    assert d == 1024 and C_GATE % d == 0 and o_gate - o_gq == C_CONF + 2 * BRANCH_W - C_GQ
    w_a = jnp.concatenate([w_in[:, :, :o_gq], jnp.zeros((depth, d, C_GQ - o_gq), F32),
                           w_in[:, :, o_gq:o_gate], jnp.zeros((depth, d, C_GATE - C_CONF - 2 * BRANCH_W), F32),
                           w_in[:, :, o_gate:]], axis=-1).astype(BF16)
    w_br = w_branch.astype(BF16)
    w_o = w_out.astype(BF16)
    w_up = ffn_w_up.astype(BF16)
    w_dn = ffn_w_down.astype(BF16)

    def row3(p):
        return p.reshape(depth, 1, -1)

    def pad_lanes(p):
        flat = p.reshape(depth, 1, -1)
        return jnp.pad(flat, ((0, 0), (0, 0), (0, LANES - flat.shape[-1])))

    qn = row3(jnp.concatenate([gqa_q_norm, gqa_q_norm], axis=-1))
    kn = row3(jnp.concatenate([gqa_k_norm, gqa_k_norm], axis=-1))
    skip = jnp.repeat(ssd_d, SSD_HEAD_DIM, axis=-1).reshape(depth, 2, 1, SSD_HEADS * SSD_HEAD_DIM)
    rope = _rope_tables(seq)

    cvec = jnp.concatenate([c, c_ctx[None, :], jnp.zeros((8 - n_batch - 1, d), F32)], axis=0)
    mod = _modulation(cvec, w_mod, b_mod).reshape(depth, 8, 6, d)
    tiles_l = seq // tm_l

    def row_latent(t):
        return t // tiles_l

    def row_ctx(t):
        return n_batch

    xl = x.reshape(n_batch * seq, d)
    xc = ctx.reshape(n_batch * n_ctx, d)
    for l in range(depth):
        need_ctx = l < depth - 1
        lambda_init = 0.8 - 0.6 * math.exp(-0.3 * l)
        proj = (row3(mix_norm_pre), w_a, qn, kn, ssd_conv_w, row3(ssd_conv_b), conf_conv_w, row3(conf_conv_b),
                row3(conf_ln_g), row3(conf_ln_b))
        qa, kat, va, z, sx, dt, gq, gkt, gv, cf = _inproj(xl, seq, tm_l, l, mod, row_latent, rope, *proj)
        qa_c, kat_c, va_c, z_c, sx_c, dt_c, gq_c, gkt_c, gv_c, cf_c = _inproj(
            xc, n_ctx, n_ctx, l, mod, row_ctx, None, *proj)
        oa = _diff_attention(qa, tq, [(kat, va), (kat_c, va_c)], da_lambda, row3(da_subln), l, lambda_init)
        og = _gqa_attention(gq, tq, [(gkt, gv), (gkt_c, gv_c)])
        yf, yb, yf_c, yb_c = _ssd(sx, dt, sx_c, dt_c, pad_lanes(ssd_a_log), pad_lanes(ssd_dt_bias), skip, l, n_batch)
        mix = (row3(mix_norm_pre), row3(mix_norm_post))
        mixw = (w_a, w_br, w_o)
        ffn = (row3(ffn_norm_pre), row3(ffn_norm_post), w_up, ffn_conv_w, row3(ffn_conv_b), w_dn)
        xl = _merge(xl, tm_l, l, mod, row_latent, *mix, oa, yf, yb, z, row3(ssd_norm), og, cf, *mixw)
        xl = _ffn(xl, seq, tm_l, l, mod, row_latent, *ffn)
        if need_ctx:
            oa_c = _diff_attention(qa_c, n_ctx, [(kat_c, va_c)], da_lambda, row3(da_subln), l, lambda_init)
            og_c = _gqa_attention(gq_c, n_ctx, [(gkt_c, gv_c)])
            xc = _merge(xc, n_ctx, l, mod, row_ctx, *mix, oa_c, yf_c, yb_c, z_c, row3(ssd_norm), og_c, cf_c, *mixw)
            xc = _ffn(xc, n_ctx, n_ctx, l, mod, row_ctx, *ffn)
    return xl.reshape(n_batch, seq, d)
```

```python
import functools
import math

import jax
import jax.numpy as jnp
from jax import lax
from jax.experimental import pallas as pl
from jax.experimental.pallas import tpu as pltpu

F32 = jnp.float32
BF16 = jnp.bfloat16

GRID_W = 64
ROPE_BASE = 10000.0
EPS = 1e-6
N_BRANCH = 4
BRANCH_W = 512
DA_HEADS = 4
DA_HEAD_DIM = 64
SSD_HEADS = 8
SSD_HEAD_DIM = 64
SSD_GROUPS = 2
SSD_HPG = SSD_HEADS // SSD_GROUPS
SSD_STATE = 128
SSD_CONV = 5
GQA_HEADS = 8
GQA_KV_HEADS = 2
GQA_HPG = GQA_HEADS // GQA_KV_HEADS
GQA_HEAD_DIM = 64
CONF_CONV = 31
FFN_CONV = 3

LANES = 128
SUBLANES = 8
TM_LATENT = 512
CHUNK = 256
HALO_A = 16
HALO_F = 8
TQ = 1024
KEY_CHUNK = 2048
DA_SUB = 512
GQA_SUB = 256
VMEM_LIMIT = 56 * 1024 * 1024

C_DAQ, C_DAK, C_DAV, C_Z, C_XBC, C_DT, C_GQ, C_GK, C_GV, C_CONF, C_GATE, C_END = (
    0, 512, 1024, 1536, 2048, 3072, 3200, 3712, 3840, 3968, 5120, 9216)
LOG2E = math.log2(math.e)


def _params(n_axes):
    return pltpu.CompilerParams(dimension_semantics=("arbitrary",) * n_axes, vmem_limit_bytes=VMEM_LIMIT)


def _resident(shape, index_map):
    return pl.BlockSpec(shape, index_map, pipeline_mode=pl.Buffered(1))


def _sigmoid(v):
    return 1.0 / (1.0 + jnp.exp(-v))


def _silu(v):
    return v * _sigmoid(v)


def _rms(v, gain):
    return v * lax.rsqrt(jnp.mean(v * v, axis=-1, keepdims=True) + EPS) * gain


def _dot(a, b):
    return jnp.dot(a, b, preferred_element_type=F32)


def _dot_nt(a, b):
    return lax.dot_general(a, b, (((1,), (1,)), ((), ())), preferred_element_type=F32)


def _split2(v):
    hi = v.astype(BF16)
    lo = (v - hi.astype(F32)).astype(BF16)
    return hi, lo


def _mod_kernel(c_ref, w_ref, b_ref, o_ref):
    s = _silu(c_ref[...])
    hi, lo = _split2(s)
    lhs = jnp.concatenate([hi, lo], axis=0)
    r = _dot(lhs, w_ref[...].astype(BF16))
    o_ref[...] = r[0:8] + r[8:16] + b_ref[...]


def _modulation(cvec, w_mod, b_mod):
    depth, d, n = w_mod.shape
    tn = 1536
    return pl.pallas_call(
        _mod_kernel,
        grid=(depth, n // tn),
        in_specs=[pl.BlockSpec((8, d), lambda l, j: (0, 0)),
                  pl.BlockSpec((None, d, tn), lambda l, j: (l, 0, j)),
                  pl.BlockSpec((None, 1, tn), lambda l, j: (l, 0, j))],
        out_specs=pl.BlockSpec((None, 8, tn), lambda l, j: (l, 0, j)),
        out_shape=jax.ShapeDtypeStruct((depth, 8, n), F32),
        compiler_params=_params(2),
        name="modulation",
    )(cvec, w_mod, b_mod.reshape(depth, 1, n))


def _row_specs(rows, d, tm, halo):
    hb = tm // halo
    last = rows // halo - 1
    return [pl.BlockSpec((tm, d), lambda t: (t, 0)),
            pl.BlockSpec((halo, d), lambda t: (jnp.maximum(t * hb - 1, 0), 0)),
            pl.BlockSpec((halo, d), lambda t: (jnp.minimum((t + 1) * hb, last), 0))]


def _mod_spec(layer, d, mod_row):
    return pl.BlockSpec((None, None, 6, d), lambda t: (layer, mod_row(t), 0, 0))


def _layer_spec(layer, shape):
    return _resident((None,) + shape, lambda t: (layer,) + (0,) * len(shape))


def _rope128(tb, cos, sin, low32):
    up = pltpu.roll(tb, LANES - 32, axis=1)
    dn = pltpu.roll(tb, 32, axis=1)
    return tb * cos + jnp.where(low32, up, dn) * sin


def _inproj_kernel(tm, tiles_per_seg, use_rope, *refs):
    if use_rope:
        (xm, xp, xn, mod, gpre, w, qn, kn, scw, scb, ccw, ccb, lng, lnb, cos_r, sin_r,
         qa_o, kat_o, va_o, z_o, xc_o, dt_o, gq_o, gkt_o, gv_o, cf_o, sbuf, rbuf) = refs
    else:
        (xm, xp, xn, mod, gpre, w, qn, kn, scw, scb, ccw, ccb, lng, lnb,
         qa_o, kat_o, va_o, z_o, xc_o, dt_o, gq_o, gkt_o, gv_o, cf_o, sbuf, rbuf) = refs
    ti = pl.program_id(0) % tiles_per_seg
    prev_ok = ti > 0
    next_ok = ti < tiles_per_seg - 1
    shift, scale, gain = mod[0:1, :], mod[1:2, :], gpre[...]

    def normmod(v):
        return _rms(v, gain) * (1.0 + scale) + shift

    hm = normmod(xm[...]).astype(BF16)
    hp = jnp.where(prev_ok, normmod(xp[...]), 0.0).astype(BF16)
    hn = jnp.where(next_ok, normmod(xn[...]), 0.0).astype(BF16)
    hext = jnp.concatenate([hp, hm, hn], axis=0)

    lane = lax.broadcasted_iota(jnp.int32, (tm, LANES), 1)
    low32 = (lane % 64) < 32
    gi = lax.broadcasted_iota(jnp.int32, (LANES, LANES), 0) // 64
    gj = lax.broadcasted_iota(jnp.int32, (LANES, LANES), 1) // 64
    gsum = (gi == gj).astype(BF16)

    def rope(tb):
        return _rope128(tb, cos_r[...], sin_r[...], low32) if use_rope else tb

    def headnorm(tb, g):
        ss = _dot((tb * tb).astype(BF16), gsum)
        return tb * lax.rsqrt(ss * (1.0 / GQA_HEAD_DIM) + EPS) * g

    def halves(c0):
        r = _dot(hm, w[:, c0:c0 + 2 * LANES])
        return r[:, 0:LANES], r[:, LANES:2 * LANES]

    a = _dot(hext, w[:, C_CONF:C_CONF + BRANCH_W])
    g = _dot(hext, w[:, C_CONF + BRANCH_W:C_CONF + 2 * BRANCH_W])
    sbuf[...] = _dot(hext, w[:, C_XBC:C_DT])
    rbuf[0] = a * _sigmoid(g)
    span = tm + 2 * HALO_A - SUBLANES
    for r in range(1, SUBLANES):
        rbuf[r, 0:span, :] = rbuf[0, pl.ds(r, span), :]
    acc = ccb[...]
    for k_tap in range(CONF_CONV):
        off = HALO_A - CONF_CONV // 2 + k_tap
        acc = acc + ccw[k_tap:k_tap + 1, :] * rbuf[off % SUBLANES, pl.ds(off - off % SUBLANES, tm), :]
    xc = acc - jnp.mean(acc, axis=-1, keepdims=True)
    y = xc * lax.rsqrt(jnp.mean(xc * xc, axis=-1, keepdims=True) + EPS) * lng[...] + lnb[...]
    cf_o[...] = _silu(y).astype(BF16)

    acc = scb[...] + scw[0:1, :] * sbuf[pl.ds(HALO_A - SSD_CONV // 2, tm), :]
    for k_tap in range(1, SSD_CONV):
        acc = acc + scw[k_tap:k_tap + 1, :] * sbuf[pl.ds(HALO_A - SSD_CONV // 2 + k_tap, tm), :]
    xc_o[...] = _silu(acc).astype(BF16)

    for j in range(2):
        for i, q in enumerate(halves(C_DAQ + j * 2 * LANES)):
            blk = 2 * j + i
            qa_o[:, blk * LANES:(blk + 1) * LANES] = (rope(q) * (DA_HEAD_DIM ** -0.5 * LOG2E)).astype(BF16)
        for i, k in enumerate(halves(C_DAK + j * 2 * LANES)):
            blk = 2 * j + i
            kat_o[blk * LANES:(blk + 1) * LANES, :] = rope(k).T.astype(BF16)
    va_o[...] = _dot(hm, w[:, C_DAV:C_Z]).astype(BF16)
    z_o[...] = _dot(hm, w[:, C_Z:C_XBC]).astype(BF16)
    dt_o[...] = _dot(hm, w[:, C_DT:C_GQ])

    for j in range(2):
        for i, q in enumerate(halves(C_GQ + j * 2 * LANES)):
            blk = 2 * j + i
            q = rope(headnorm(q, qn[...]))
            gq_o[:, blk * LANES:(blk + 1) * LANES] = (q * (GQA_HEAD_DIM ** -0.5 * LOG2E)).astype(BF16)
    k, v = halves(C_GK)
    gkt_o[...] = rope(headnorm(k, kn[...])).T.astype(BF16)
    gv_o[...] = v.astype(BF16)


def _inproj(xs, seg_len, tm, layer, mod, mod_row, rope, gpre, w_a, qn, kn, scw, scb, ccw, ccb, lng, lnb):
    rows, d = xs.shape
    tps = seg_len // tm
    n_seg = rows // seg_len

    def out(width, dtype):
        return pl.BlockSpec((tm, width), lambda t: (t, 0)), jax.ShapeDtypeStruct((rows, width), dtype)

    def out_t(width):
        return (pl.BlockSpec((None, width, tm), lambda t: (t // tps, 0, t % tps)),
                jax.ShapeDtypeStruct((n_seg, width, seg_len), BF16))

    outs = [out(512, BF16), out_t(512), out(512, BF16), out(512, BF16), out(1024, BF16), out(LANES, F32),
            out(512, BF16), out_t(LANES), out(LANES, BF16), out(512, BF16)]
    in_specs = _row_specs(rows, d, tm, HALO_A) + [
        _mod_spec(layer, d, mod_row), _layer_spec(layer, (1, d)), _layer_spec(layer, (d, C_GATE)),
        _layer_spec(layer, (1, LANES)), _layer_spec(layer, (1, LANES)),
        _layer_spec(layer, (SSD_CONV, 1024)), _layer_spec(layer, (1, 1024)),
        _layer_spec(layer, (CONF_CONV, BRANCH_W)), _layer_spec(layer, (1, BRANCH_W)),
        _layer_spec(layer, (1, BRANCH_W)), _layer_spec(layer, (1, BRANCH_W))]
    args = [xs, xs, xs, mod, gpre, w_a, qn, kn, scw, scb, ccw, ccb, lng, lnb]
    if rope is not None:
        in_specs += [pl.BlockSpec((tm, LANES), lambda t: (t % tps, 0))] * 2
        args += list(rope)
    return pl.pallas_call(
        functools.partial(_inproj_kernel, tm, tps, rope is not None),
        grid=(rows // tm,),
        in_specs=in_specs,
        out_specs=[o[0] for o in outs],
        out_shape=[o[1] for o in outs],
        scratch_shapes=[pltpu.VMEM((tm + 2 * HALO_A, 1024), F32),
                        pltpu.VMEM((SUBLANES, tm + 2 * HALO_A, BRANCH_W), F32)],
        compiler_params=_params(1),
        name="inproj",
    )(*args)


def _key_chunks(seg_lens):
    return [(s, off, min(KEY_CHUNK, n - off)) for s, n in enumerate(seg_lens) for off in range(0, n, KEY_CHUNK)]


def _online_softmax_pv(score_fn, value_fn, chunks):
    m, acc = None, None
    ahead = [score_fn(c) for c in chunks[:2]]
    for i, c in enumerate(chunks):
        s = ahead.pop(0)
        if i + 2 < len(chunks):
            ahead.append(score_fn(chunks[i + 2]))
        mc = jnp.max(s, axis=-1, keepdims=True)
        if m is None:
            m = mc
            acc = _dot(jnp.exp2(s - m).astype(BF16), value_fn(c))
        else:
            m_new = jnp.maximum(m, mc)
            acc = acc * jnp.exp2(m - m_new) + _dot(jnp.exp2(s - m_new).astype(BF16), value_fn(c))
            m = m_new
    return acc


def _da_kernel(lambda_init, sub, seg_lens, q_ref, *refs):
    n_seg = len(seg_lens)
    kt_refs, v_refs = refs[0:2 * n_seg:2], refs[1:2 * n_seg:2]
    lam_ref, sub_ref, o_ref = refs[2 * n_seg:]
    chunks = _key_chunks(seg_lens)
    lp = lam_ref[...]
    lam = (jnp.exp(jnp.sum(lp[0:1] * lp[1:2], axis=-1, keepdims=True))
           - jnp.exp(jnp.sum(lp[2:3] * lp[3:4], axis=-1, keepdims=True)) + lambda_init)
    lane = lax.broadcasted_iota(jnp.int32, (sub, LANES), 1)

    def tile(i, carry):
        rs = pl.ds(pl.multiple_of(i * sub, sub), sub)
        for h in range(DA_HEADS):
            blk, half = h // 2, h % 2
            keep = (lane // 64) == half
            q1 = q_ref[rs, blk * LANES:(blk + 1) * LANES]
            q2 = q_ref[rs, (2 + blk) * LANES:(3 + blk) * LANES]
            q1 = jnp.where(keep, q1, jnp.zeros_like(q1))
            q2 = jnp.where(keep, q2, jnp.zeros_like(q2))

            def scores(c):
                s, off, n = c
                s1 = _dot(q1, kt_refs[s][blk * LANES:(blk + 1) * LANES, off:off + n])
                s2 = _dot(q2, kt_refs[s][(2 + blk) * LANES:(3 + blk) * LANES, off:off + n])
                return jnp.concatenate([s1, s2], axis=0)

            def values(c):
                s, off, n = c
                v = v_refs[s][off:off + n, h * LANES:(h + 1) * LANES]
                return jnp.concatenate([v, jnp.ones_like(v)], axis=1)

            acc = _online_softmax_pv(scores, values, chunks)
            o1 = acc[0:sub, 0:LANES] / acc[0:sub, LANES:2 * LANES]
            o2 = acc[sub:2 * sub, 0:LANES] / acc[sub:2 * sub, LANES:2 * LANES]
            o = _rms(o1 - lam * o2, sub_ref[...]) * (1.0 - lambda_init)
            o_ref[rs, h * LANES:(h + 1) * LANES] = o.astype(BF16)
        return carry

    lax.fori_loop(0, q_ref.shape[0] // sub, tile, 0)


def _diff_attention(q, tq, segs, da_lambda, subln, layer, lambda_init):
    rows = q.shape[0]
    n_batch = segs[0][0].shape[0]
    nq = rows // n_batch // tq
    seg_lens = tuple(kt.shape[2] for kt, _ in segs)
    in_specs = [pl.BlockSpec((tq, 512), lambda b, i: (b * nq + i, 0))]
    args = [q]
    for kt, v in segs:
        n = kt.shape[2]
        in_specs += [pl.BlockSpec((None, 512, n), lambda b, i: (b, 0, 0)),
                     pl.BlockSpec((None, n, 512), lambda b, i: (b, 0, 0))]
        args += [kt, v.reshape(n_batch, n, 512)]
    in_specs += [pl.BlockSpec((None, 4, DA_HEAD_DIM), lambda b, i: (layer, 0, 0)),
                 pl.BlockSpec((None, 1, LANES), lambda b, i: (layer, 0, 0))]
    return pl.pallas_call(
        functools.partial(_da_kernel, lambda_init, min(DA_SUB, tq), seg_lens),
        grid=(n_batch, nq),
        in_specs=in_specs,
        out_specs=pl.BlockSpec((tq, 512), lambda b, i: (b * nq + i, 0)),
        out_shape=jax.ShapeDtypeStruct((rows, 512), BF16),
        compiler_params=_params(2),
        name="diff_attention",
    )(*args, da_lambda, subln)


def _gqa_kernel(sub, seg_lens, q_ref, *refs):
    n_seg = len(seg_lens)
    kt_refs, v_refs = refs[0:2 * n_seg:2], refs[1:2 * n_seg:2]
    o_ref = refs[2 * n_seg]
    chunks = _key_chunks(seg_lens)
    lane = lax.broadcasted_iota(jnp.int32, (sub, LANES), 1)
    lane_st = lax.broadcasted_iota(jnp.int32, (GQA_HPG * sub, LANES), 1)

    def tile(i, carry):
        rs = pl.ds(pl.multiple_of(i * sub, sub), sub)
        for g in range(GQA_KV_HEADS):
            in_group = (lane // 64) == g
            stack = []
            for r in range(GQA_HPG):
                head = g * GQA_HPG + r
                qb = q_ref[rs, (head // 2) * LANES:(head // 2 + 1) * LANES].astype(F32)
                if head % 2 != g:
                    qb = pltpu.roll(qb, 64, axis=1)
                stack.append(jnp.where(in_group, qb, 0.0).astype(BF16))
            qst = jnp.concatenate(stack, axis=0)

            def scores(c):
                s, off, n = c
                return _dot(qst, kt_refs[s][:, off:off + n])

            def values(c):
                s, off, n = c
                v = v_refs[s][off:off + n, :]
                vl = lax.broadcasted_iota(jnp.int32, v.shape, 1)
                return jnp.where((vl // 64) == g, v, jnp.ones_like(v))

            acc = _online_softmax_pv(scores, values, chunks)
            den = jnp.where((lane_st // 64) == g, pltpu.roll(acc, 64, axis=1), 1.0)
            o = acc / den
            for pair in range(GQA_HPG // 2):
                pieces = []
                for half in range(2):
                    r = 2 * pair + half
                    piece = o[r * sub:(r + 1) * sub]
                    pieces.append(piece if half == g else pltpu.roll(piece, 64, axis=1))
                blk = (g * GQA_HPG) // 2 + pair
                o_ref[rs, blk * LANES:(blk + 1) * LANES] = jnp.where(lane < 64, pieces[0], pieces[1]).astype(BF16)
        return carry

    lax.fori_loop(0, q_ref.shape[0] // sub, tile, 0)


def _gqa_attention(q, tq, segs):
    rows = q.shape[0]
    n_batch = segs[0][0].shape[0]
    nq = rows // n_batch // tq
    seg_lens = tuple(kt.shape[2] for kt, _ in segs)
    in_specs = [pl.BlockSpec((tq, 512), lambda b, i: (b * nq + i, 0))]
    args = [q]
    for kt, v in segs:
        n = kt.shape[2]
        in_specs += [pl.BlockSpec((None, LANES, n), lambda b, i: (b, 0, 0)),
                     pl.BlockSpec((None, n, LANES), lambda b, i: (b, 0, 0))]
        args += [kt, v.reshape(n_batch, n, LANES)]
    return pl.pallas_call(
        functools.partial(_gqa_kernel, min(GQA_SUB, tq), seg_lens),
        grid=(n_batch, nq),
        in_specs=in_specs,
        out_specs=pl.BlockSpec((tq, 512), lambda b, i: (b * nq + i, 0)),
        out_shape=jax.ShapeDtypeStruct((rows, 512), BF16),
        compiler_params=_params(2),
        name="gqa_attention",
    )(*args)


def _ssd_direction(d, x_ref, dt_ref, alog_ref, bias_ref, skip_ref, y_ref, h_ref):
    q = CHUNK
    lane = lax.broadcasted_iota(jnp.int32, (1, LANES), 1)
    a_vec = jnp.where(lane < 2 * SSD_HEADS, -jnp.exp(alog_ref[...]), 0.0)
    raw = dt_ref[...] + bias_ref[...]
    dt = jnp.maximum(raw, 0.0) + jnp.log1p(jnp.exp(-jnp.abs(raw)))
    da = dt * a_vec
    ri = lax.broadcasted_iota(jnp.int32, (q, q), 0)
    ci = lax.broadcasted_iota(jnp.int32, (q, q), 1)
    tri = (ci <= ri) if d == 0 else (ci >= ri)
    hi = da.astype(BF16).astype(F32)
    rest = da - hi
    mid = rest.astype(BF16).astype(F32)
    packed = hi + pltpu.roll(mid, 2 * SSD_HEADS, axis=1) + pltpu.roll(rest - mid, 4 * SSD_HEADS, axis=1)
    part = _dot(tri.astype(BF16), packed.astype(BF16))
    cs = part + pltpu.roll(part, LANES - 2 * SSD_HEADS, axis=1) + pltpu.roll(part, LANES - 4 * SSD_HEADS, axis=1)
    cs = jnp.where(lane < 2 * SSD_HEADS, cs, 0.0)
    cs_t = cs.T
    tot = jnp.sum(da, axis=0, keepdims=True)

    er = lax.broadcasted_iota(jnp.int32, (2 * LANES, 512), 0) % LANES
    ec = lax.broadcasted_iota(jnp.int32, (2 * LANES, 512), 1) // SSD_HEAD_DIM
    expand = (er == d * SSD_HEADS + ec).astype(BF16)
    stack = jnp.concatenate([dt, jnp.exp(tot - cs), jnp.exp(cs), jnp.broadcast_to(jnp.exp(tot), (8, LANES))], axis=0)
    hi, lo = _split2(stack)
    wide = _dot(jnp.concatenate([hi, lo], axis=1), expand)
    dt_e, dte_e, ecs_e, cdec_e = wide[0:q], wide[q:2 * q], wide[2 * q:3 * q], wide[3 * q:3 * q + 1]

    xs = x_ref[:, 0:512].astype(F32)
    xsdt = xs * dt_e
    lane_g = lax.broadcasted_iota(jnp.int32, (q, 2 * LANES), 1) // SSD_HEAD_DIM
    for g in range(SSD_GROUPS):
        gl = slice(g * 2 * LANES, (g + 1) * 2 * LANES)
        bm = x_ref[:, 512 + g * LANES:512 + (g + 1) * LANES].astype(F32)
        cm = x_ref[:, 768 + g * LANES:768 + (g + 1) * LANES]
        cb = _dot_nt(cm, bm.astype(BF16))
        xg = xsdt[:, gl]
        y = _dot(cm, h_ref[g].astype(BF16)) * ecs_e[:, gl]
        for r in range(SSD_HPG):
            idx = d * SSD_HEADS + g * SSD_HPG + r
            seg = cs[:, idx:idx + 1] - cs_t[idx:idx + 1, :]
            lmat = jnp.exp(jnp.where(tri, seg, -jnp.inf))
            y = y + _dot((cb * lmat).astype(BF16), jnp.where(lane_g == r, xg, 0.0).astype(BF16))
        y_ref[:, gl] = y + skip_ref[:, gl] * xs[:, gl]
        h_ref[g] = cdec_e[:, gl] * h_ref[g] + _dot(bm.T.astype(BF16), (xg * dte_e[:, gl]).astype(BF16))


def _ssd_kernel(xf, dtf, xb, dtb, xc, dtc, alog, bias, skip, yf, yb, yfc, ybc, h_ref):
    t = pl.program_id(1)

    @pl.when(t == 0)
    def _():
        h_ref[...] = jnp.zeros_like(h_ref)
        yf[...] = jnp.zeros_like(yf)
        yb[...] = jnp.zeros_like(yb)
        _ssd_direction(0, xc, dtc, alog, bias, skip.at[0], yfc, h_ref.at[0])
        _ssd_direction(1, xc, dtc, alog, bias, skip.at[1], ybc, h_ref.at[1])

    @pl.when(t != 0)
    def _():
        _ssd_direction(0, xf, dtf, alog, bias, skip.at[0], yf, h_ref.at[0])
        _ssd_direction(1, xb, dtb, alog, bias, skip.at[1], yb, h_ref.at[1])


def _ssd(xc_l, dt_l, xc_c, dt_c, alog, bias, skip, layer, n_batch):
    rows_l, rows_c = xc_l.shape[0], xc_c.shape[0]
    n_l = rows_l // n_batch // CHUNK

    def fwd(b, t):
        return (b * n_l + jnp.maximum(t - 1, 0), 0)

    def bwd(b, t):
        return (b * n_l + n_l - jnp.maximum(t, 1), 0)

    def cxt(b, t):
        return (b, 0)

    def lyr(shape):
        return pl.BlockSpec((None,) + shape, lambda b, t: (layer,) + (0,) * len(shape))

    return pl.pallas_call(
        _ssd_kernel,
        grid=(n_batch, n_l + 1),
        in_specs=[pl.BlockSpec((CHUNK, 1024), fwd), pl.BlockSpec((CHUNK, LANES), fwd),
                  pl.BlockSpec((CHUNK, 1024), bwd), pl.BlockSpec((CHUNK, LANES), bwd),
                  pl.BlockSpec((CHUNK, 1024), cxt), pl.BlockSpec((CHUNK, LANES), cxt),
                  lyr((1, LANES)), lyr((1, LANES)), lyr((2, 1, 512))],
        out_specs=[pl.BlockSpec((CHUNK, 512), fwd), pl.BlockSpec((CHUNK, 512), bwd),
                   pl.BlockSpec((CHUNK, 512), cxt), pl.BlockSpec((CHUNK, 512), cxt)],
        out_shape=[jax.ShapeDtypeStruct((rows_l, 512), F32)] * 2 + [jax.ShapeDtypeStruct((rows_c, 512), F32)] * 2,
        scratch_shapes=[pltpu.VMEM((2, SSD_GROUPS, SSD_STATE, 2 * LANES), F32)],
        compiler_params=_params(2),
        name="ssd_scan",
    )(xc_l, dt_l, xc_l, dt_l, xc_c, dt_c, alog, bias, skip)


def _merge_kernel(x_ref, mod, gpre, gpost, oa, yf, yb, z, ssdn, og, cf, wg0, wg1, wg2, wg3, wb, wo, o_ref):
    x = x_ref[...]
    h = (_rms(x, gpre[...]) * (1.0 + mod[1:2, :]) + mod[0:1, :]).astype(BF16)
    ys = _rms((yf[...] + yb[...]) * _silu(z[...].astype(F32)), ssdn[...]).astype(BF16)
    branches = [oa[...], ys, og[...], cf[...]]
    gates = [wg0, wg1, wg2, wg3]
    mixed = None
    for k in range(N_BRANCH):
        term = _sigmoid(_dot(h, gates[k][...])) * _dot(branches[k], wb[k])
        mixed = term if mixed is None else mixed + term
    out = _dot(mixed.astype(BF16), wo[...])
    o_ref[...] = x + mod[2:3, :] * _rms(out, gpost[...])


def _merge(xs, tm, layer, mod, mod_row, gpre, gpost, oa, yf, yb, z, ssdn, og, cf, w_all, w_branch, w_out):
    rows, d = xs.shape

    def tile(width):
        return pl.BlockSpec((tm, width), lambda t: (t, 0))

    return pl.pallas_call(
        _merge_kernel,
        grid=(rows // tm,),
        in_specs=[tile(d), _mod_spec(layer, d, mod_row), _layer_spec(layer, (1, d)), _layer_spec(layer, (1, d)),
                  tile(512), tile(512), tile(512), tile(512), _layer_spec(layer, (1, BRANCH_W)), tile(512), tile(512),
                  *[_resident((None, d, d), lambda t, k=k: (layer, 0, C_GATE // d + k)) for k in range(N_BRANCH)],
                  _layer_spec(layer, (N_BRANCH, BRANCH_W, d)),
                  _layer_spec(layer, (d, d))],
        out_specs=tile(d),
        out_shape=jax.ShapeDtypeStruct((rows, d), F32),
        compiler_params=_params(1),
        name="merge",
    )(xs, mod, gpre, gpost, oa, yf, yb, z, ssdn, og, cf, w_all, w_all, w_all, w_all, w_branch, w_out)


FFN_COL_CHUNK = 512


def _ffn_chunks(d_ff):
    return [(c0, min(FFN_COL_CHUNK, d_ff - c0)) for c0 in range(0, d_ff, FFN_COL_CHUNK)]


def _ffn_kernel(tm, tiles_per_seg, d_ff, xm, xp, xn, mod, gpre, gpost, wu, cw_ref, cb_ref, wd, o_ref, abuf, gbuf):
    ti = pl.program_id(0) % tiles_per_seg
    prev_ok = ti > 0
    next_ok = ti < tiles_per_seg - 1
    shift, scale, gain = mod[3:4, :], mod[4:5, :], gpre[...]

    def normmod(v):
        return _rms(v, gain) * (1.0 + scale) + shift

    x = xm[...]
    hext = jnp.concatenate([jnp.where(prev_ok, normmod(xp[...]), 0.0), normmod(x),
                            jnp.where(next_ok, normmod(xn[...]), 0.0)], axis=0).astype(BF16)

    def conv(buf, c0, cw):
        acc = cb_ref[:, c0:c0 + cw] + cw_ref[0:1, c0:c0 + cw] * buf[pl.ds(HALO_F - 1, tm), 0:cw]
        for k in range(1, FFN_CONV):
            acc = acc + cw_ref[k:k + 1, c0:c0 + cw] * buf[pl.ds(HALO_F - 1 + k, tm), 0:cw]
        return acc

    chunks = _ffn_chunks(d_ff)

    def up(i):
        c0, cw = chunks[i]
        abuf[i % 3, :, 0:cw] = _dot(hext, wu[:, c0:c0 + cw])
        gbuf[i % 3, :, 0:cw] = _dot(hext, wu[:, d_ff + c0:d_ff + c0 + cw])

    up(0)
    up(1)
    out = None
    for i, (c0, cw) in enumerate(chunks):
        if i + 2 < len(chunks):
            up(i + 2)
        act = (_silu(conv(gbuf.at[i % 3], d_ff + c0, cw)) * conv(abuf.at[i % 3], c0, cw)).astype(BF16)
        part = _dot(act, wd[c0:c0 + cw, :])
        out = part if out is None else out + part
    o_ref[...] = x + mod[5:6, :] * _rms(out, gpost[...])


def _ffn(xs, seg_len, tm, layer, mod, mod_row, gpre, gpost, w_up, conv_w, conv_b, w_down):
    rows, d = xs.shape
    d_ff = w_down.shape[1]
    return pl.pallas_call(
        functools.partial(_ffn_kernel, tm, seg_len // tm, d_ff),
        grid=(rows // tm,),
        in_specs=_row_specs(rows, d, tm, HALO_F) + [
            _mod_spec(layer, d, mod_row), _layer_spec(layer, (1, d)), _layer_spec(layer, (1, d)),
            _layer_spec(layer, (d, 2 * d_ff)), _layer_spec(layer, (FFN_CONV, 2 * d_ff)),
            _layer_spec(layer, (1, 2 * d_ff)), _layer_spec(layer, (d_ff, d))],
        out_specs=pl.BlockSpec((tm, d), lambda t: (t, 0)),
        out_shape=jax.ShapeDtypeStruct((rows, d), F32),
        scratch_shapes=[pltpu.VMEM((3, tm + 2 * HALO_F, FFN_COL_CHUNK), F32)] * 2,
        compiler_params=_params(1),
        name="conv_ffn",
    )(xs, xs, xs, mod, gpre, gpost, w_up, conv_w, conv_b, w_down)


def _rope_tables(seq):
    rows = seq // GRID_W
    row = jnp.repeat(jnp.arange(rows, dtype=F32), GRID_W)
    col = jnp.tile(jnp.arange(GRID_W, dtype=F32), rows)
    quarter = DA_HEAD_DIM // 4
    inv_freq = ROPE_BASE ** (-jnp.arange(quarter, dtype=F32) / quarter)
    ang = jnp.concatenate([row[:, None] * inv_freq, col[:, None] * inv_freq], axis=-1)
    cos, sin = jnp.cos(ang), jnp.sin(ang)
    return jnp.concatenate([cos] * 4, axis=-1), jnp.concatenate([-sin, sin, -sin, sin], axis=-1)


def kernel(x, c, ctx, c_ctx, w_mod, b_mod, mix_norm_pre, mix_norm_post, w_in, da_lambda, da_subln, ssd_conv_w, ssd_conv_b, ssd_a_log, ssd_dt_bias, ssd_d, ssd_norm, gqa_q_norm, gqa_k_norm, conf_conv_w, conf_conv_b, conf_ln_g, conf_ln_b, w_branch, w_out, ffn_norm_pre, ffn_norm_post, ffn_w_up, ffn_conv_w, ffn_conv_b, ffn_w_down):
    n_batch, seq, d = x.shape
    n_ctx = ctx.shape[1]
    depth = w_mod.shape[0]
    tm_l = min(TM_LATENT, seq)
    tq = min(TQ, seq)
    assert n_ctx == CHUNK and seq % tm_l == 0 and seq % tq == 0 and seq % GRID_W == 0 and n_batch < 8

    o_gq = C_DT + 2 * SSD_HEADS
    o_gate = w_in.shape[-1] - N_BRANCH * d
    assert d == 1024 and C_GATE % d == 0 and o_gate - o_gq == C_CONF + 2 * BRANCH_W - C_GQ
    w_a = jnp.concatenate([w_in[:, :, :o_gq], jnp.zeros((depth, d, C_GQ - o_gq), F32),
                           w_in[:, :, o_gq:o_gate], jnp.zeros((depth, d, C_GATE - C_CONF - 2 * BRANCH_W), F32),
                           w_in[:, :, o_gate:]], axis=-1).astype(BF16)
    w_br = w_branch.astype(BF16)
    w_o = w_out.astype(BF16)
    w_up = ffn_w_up.astype(BF16)
    w_dn = ffn_w_down.astype(BF16)

    def row3(p):
        return p.reshape(depth, 1, -1)

    def pad_lanes(p):
        flat = p.reshape(depth, 1, -1)
        return jnp.pad(flat, ((0, 0), (0, 0), (0, LANES - flat.shape[-1])))

    qn = row3(jnp.concatenate([gqa_q_norm, gqa_q_norm], axis=-1))
    kn = row3(jnp.concatenate([gqa_k_norm, gqa_k_norm], axis=-1))
    skip = jnp.repeat(ssd_d, SSD_HEAD_DIM, axis=-1).reshape(depth, 2, 1, SSD_HEADS * SSD_HEAD_DIM)
    rope = _rope_tables(seq)

    cvec = jnp.concatenate([c, c_ctx[None, :], jnp.zeros((8 - n_batch - 1, d), F32)], axis=0)
    mod = _modulation(cvec, w_mod, b_mod).reshape(depth, 8, 6, d)
    tiles_l = seq // tm_l

    def row_latent(t):
        return t // tiles_l

    def row_ctx(t):
        return n_batch

    xl = x.reshape(n_batch * seq, d)
    xc = ctx.reshape(n_batch * n_ctx, d)
    for l in range(depth):
        need_ctx = l < depth - 1
        lambda_init = 0.8 - 0.6 * math.exp(-0.3 * l)
        proj = (row3(mix_norm_pre), w_a, qn, kn, ssd_conv_w, row3(ssd_conv_b), conf_conv_w, row3(conf_conv_b),
                row3(conf_ln_g), row3(conf_ln_b))
        qa, kat, va, z, sx, dt, gq, gkt, gv, cf = _inproj(xl, seq, tm_l, l, mod, row_latent, rope, *proj)
        qa_c, kat_c, va_c, z_c, sx_c, dt_c, gq_c, gkt_c, gv_c, cf_c = _inproj(
            xc, n_ctx, n_ctx, l, mod, row_ctx, None, *proj)
        oa = _diff_attention(qa, tq, [(kat, va), (kat_c, va_c)], da_lambda, row3(da_subln), l, lambda_init)
        og = _gqa_attention(gq, tq, [(gkt, gv), (gkt_c, gv_c)])
        yf, yb, yf_c, yb_c = _ssd(sx, dt, sx_c, dt_c, pad_lanes(ssd_a_log), pad_lanes(ssd_dt_bias), skip, l, n_batch)
        mix = (row3(mix_norm_pre), row3(mix_norm_post))
        mixw = (w_a, w_br, w_o)
        ffn = (row3(ffn_norm_pre), row3(ffn_norm_post), w_up, ffn_conv_w, row3(ffn_conv_b), w_dn)
        xl = _merge(xl, tm_l, l, mod, row_latent, *mix, oa, yf, yb, z, row3(ssd_norm), og, cf, *mixw)
        xl = _ffn(xl, seq, tm_l, l, mod, row_latent, *ffn)
        if need_ctx:
            oa_c = _diff_attention(qa_c, n_ctx, [(kat_c, va_c)], da_lambda, row3(da_subln), l, lambda_init)
            og_c = _gqa_attention(gq_c, n_ctx, [(gkt_c, gv_c)])
            xc = _merge(xc, n_ctx, l, mod, row_ctx, *mix, oa_c, yf_c, yb_c, z_c, row3(ssd_norm), og_c, cf_c, *mixw)
            xc = _ffn(xc, n_ctx, n_ctx, l, mod, row_ctx, *ffn)
    return xl.reshape(n_batch, seq, d)
```

```python
import functools
import math

import jax
import jax.numpy as jnp
from jax import lax
from jax.experimental import pallas as pl
from jax.experimental.pallas import tpu as pltpu

F32 = jnp.float32
BF16 = jnp.bfloat16

GRID_W = 64
ROPE_BASE = 10000.0
EPS = 1e-6
N_BRANCH = 4
BRANCH_W = 512
DA_HEADS = 4
DA_HEAD_DIM = 64
SSD_HEADS = 8
SSD_HEAD_DIM = 64
SSD_GROUPS = 2
SSD_HPG = SSD_HEADS // SSD_GROUPS
SSD_STATE = 128
SSD_CONV = 5
GQA_HEADS = 8
GQA_KV_HEADS = 2
GQA_HPG = GQA_HEADS // GQA_KV_HEADS
GQA_HEAD_DIM = 64
CONF_CONV = 31
FFN_CONV = 3

LANES = 128
SUBLANES = 8
TM_LATENT = 512
CHUNK = 256
HALO_A = 16
HALO_F = 8
TQ = 1024
KEY_CHUNK = 2048
DA_SUB = 512
GQA_SUB = 256
VMEM_LIMIT = 56 * 1024 * 1024

C_DAQ, C_DAK, C_DAV, C_Z, C_XBC, C_DT, C_GQ, C_GK, C_GV, C_CONF, C_GATE, C_END = (
    0, 512, 1024, 1536, 2048, 3072, 3200, 3712, 3840, 3968, 5120, 9216)
LOG2E = math.log2(math.e)


def _params(n_axes):
    return pltpu.CompilerParams(dimension_semantics=("arbitrary",) * n_axes, vmem_limit_bytes=VMEM_LIMIT)


def _resident(shape, index_map):
    return pl.BlockSpec(shape, index_map, pipeline_mode=pl.Buffered(1))


def _sigmoid(v):
    return 1.0 / (1.0 + jnp.exp(-v))


def _silu(v):
    return v * _sigmoid(v)


def _rms(v, gain):
    return v * lax.rsqrt(jnp.mean(v * v, axis=-1, keepdims=True) + EPS) * gain


def _dot(a, b):
    return jnp.dot(a, b, preferred_element_type=F32)


def _dot_nt(a, b):
    return lax.dot_general(a, b, (((1,), (1,)), ((), ())), preferred_element_type=F32)


def _split2(v):
    hi = v.astype(BF16)
    lo = (v - hi.astype(F32)).astype(BF16)
    return hi, lo


def _mod_kernel(c_ref, w_ref, b_ref, o_ref):
    s = _silu(c_ref[...])
    hi, lo = _split2(s)
    lhs = jnp.concatenate([hi, lo], axis=0)
    r = _dot(lhs, w_ref[...].astype(BF16))
    o_ref[...] = r[0:8] + r[8:16] + b_ref[...]


def _modulation(cvec, w_mod, b_mod):
    depth, d, n = w_mod.shape
    tn = 1536
    return pl.pallas_call(
        _mod_kernel,
        grid=(depth, n // tn),
        in_specs=[pl.BlockSpec((8, d), lambda l, j: (0, 0)),
                  pl.BlockSpec((None, d, tn), lambda l, j: (l, 0, j)),
                  pl.BlockSpec((None, 1, tn), lambda l, j: (l, 0, j))],
        out_specs=pl.BlockSpec((None, 8, tn), lambda l, j: (l, 0, j)),
        out_shape=jax.ShapeDtypeStruct((depth, 8, n), F32),
        compiler_params=_params(2),
        name="modulation",
    )(cvec, w_mod, b_mod.reshape(depth, 1, n))


def _row_specs(rows, d, tm, halo):
    hb = tm // halo
    last = rows // halo - 1
    return [pl.BlockSpec((tm, d), lambda t: (t, 0)),
            pl.BlockSpec((halo, d), lambda t: (jnp.maximum(t * hb - 1, 0), 0)),
            pl.BlockSpec((halo, d), lambda t: (jnp.minimum((t + 1) * hb, last), 0))]


def _mod_spec(layer, d, mod_row):
    return pl.BlockSpec((None, None, 6, d), lambda t: (layer, mod_row(t), 0, 0))


def _layer_spec(layer, shape):
    return _resident((None,) + shape, lambda t: (layer,) + (0,) * len(shape))


def _rope128(tb, cos, sin, low32):
    up = pltpu.roll(tb, LANES - 32, axis=1)
    dn = pltpu.roll(tb, 32, axis=1)
    return tb * cos + jnp.where(low32, up, dn) * sin


def _inproj_kernel(tm, tiles_per_seg, use_rope, *refs):
    if use_rope:
        (xm, xp, xn, mod, gpre, w, qn, kn, scw, scb, ccw, ccb, lng, lnb, cos_r, sin_r,
         qa_o, kat_o, va_o, z_o, xc_o, dt_o, gq_o, gkt_o, gv_o, cf_o, sbuf, rbuf) = refs
    else:
        (xm, xp, xn, mod, gpre, w, qn, kn, scw, scb, ccw, ccb, lng, lnb,
         qa_o, kat_o, va_o, z_o, xc_o, dt_o, gq_o, gkt_o, gv_o, cf_o, sbuf, rbuf) = refs
    ti = pl.program_id(0) % tiles_per_seg
    prev_ok = ti > 0
    next_ok = ti < tiles_per_seg - 1
    shift, scale, gain = mod[0:1, :], mod[1:2, :], gpre[...]

    def normmod(v):
        return _rms(v, gain) * (1.0 + scale) + shift

    hm = normmod(xm[...]).astype(BF16)
    hp = jnp.where(prev_ok, normmod(xp[...]), 0.0).astype(BF16)
    hn = jnp.where(next_ok, normmod(xn[...]), 0.0).astype(BF16)
    hext = jnp.concatenate([hp, hm, hn], axis=0)

    lane = lax.broadcasted_iota(jnp.int32, (tm, LANES), 1)
    low32 = (lane % 64) < 32
    gi = lax.broadcasted_iota(jnp.int32, (LANES, LANES), 0) // 64
    gj = lax.broadcasted_iota(jnp.int32, (LANES, LANES), 1) // 64
    gsum = (gi == gj).astype(BF16)

    def rope(tb):
        return _rope128(tb, cos_r[...], sin_r[...], low32) if use_rope else tb

    def headnorm(tb, g):
        ss = _dot((tb * tb).astype(BF16), gsum)
        return tb * lax.rsqrt(ss * (1.0 / GQA_HEAD_DIM) + EPS) * g

    def halves(c0):
        r = _dot(hm, w[:, c0:c0 + 2 * LANES])
        return r[:, 0:LANES], r[:, LANES:2 * LANES]

    a = _dot(hext, w[:, C_CONF:C_CONF + BRANCH_W])
    g = _dot(hext, w[:, C_CONF + BRANCH_W:C_CONF + 2 * BRANCH_W])
    sbuf[...] = _dot(hext, w[:, C_XBC:C_DT])
    rbuf[0] = a * _sigmoid(g)
    span = tm + 2 * HALO_A - SUBLANES
    for r in range(1, SUBLANES):
        rbuf[r, 0:span, :] = rbuf[0, pl.ds(r, span), :]
    acc = ccb[...]
    for k_tap in range(CONF_CONV):
        off = HALO_A - CONF_CONV // 2 + k_tap
        acc = acc + ccw[k_tap:k_tap + 1, :] * rbuf[off % SUBLANES, pl.ds(off - off % SUBLANES, tm), :]
    xc = acc - jnp.mean(acc, axis=-1, keepdims=True)
    y = xc * lax.rsqrt(jnp.mean(xc * xc, axis=-1, keepdims=True) + EPS) * lng[...] + lnb[...]
    cf_o[...] = _silu(y).astype(BF16)

    acc = scb[...] + scw[0:1, :] * sbuf[pl.ds(HALO_A - SSD_CONV // 2, tm), :]
    for k_tap in range(1, SSD_CONV):
        acc = acc + scw[k_tap:k_tap + 1, :] * sbuf[pl.ds(HALO_A - SSD_CONV // 2 + k_tap, tm), :]
    xc_o[...] = _silu(acc).astype(BF16)

    for j in range(2):
        for i, q in enumerate(halves(C_DAQ + j * 2 * LANES)):
            blk = 2 * j + i
            qa_o[:, blk * LANES:(blk + 1) * LANES] = (rope(q) * (DA_HEAD_DIM ** -0.5 * LOG2E)).astype(BF16)
        for i, k in enumerate(halves(C_DAK + j * 2 * LANES)):
            blk = 2 * j + i
            kat_o[blk * LANES:(blk + 1) * LANES, :] = rope(k).T.astype(BF16)
    va_o[...] = _dot(hm, w[:, C_DAV:C_Z]).astype(BF16)
    z_o[...] = _dot(hm, w[:, C_Z:C_XBC]).astype(BF16)
    dt_o[...] = _dot(hm, w[:, C_DT:C_GQ])

    for j in range(2):
        for i, q in enumerate(halves(C_GQ + j * 2 * LANES)):
            blk = 2 * j + i
            q = rope(headnorm(q, qn[...]))
            gq_o[:, blk * LANES:(blk + 1) * LANES] = (q * (GQA_HEAD_DIM ** -0.5 * LOG2E)).astype(BF16)
    k, v = halves(C_GK)
    gkt_o[...] = rope(headnorm(k, kn[...])).T.astype(BF16)
    gv_o[...] = v.astype(BF16)


def _inproj(xs, seg_len, tm, layer, mod, mod_row, rope, gpre, w_a, qn, kn, scw, scb, ccw, ccb, lng, lnb):
    rows, d = xs.shape
    tps = seg_len // tm
    n_seg = rows // seg_len

    def out(width, dtype):
        return pl.BlockSpec((tm, width), lambda t: (t, 0)), jax.ShapeDtypeStruct((rows, width), dtype)

    def out_t(width):
        return (pl.BlockSpec((None, width, tm), lambda t: (t // tps, 0, t % tps)),
                jax.ShapeDtypeStruct((n_seg, width, seg_len), BF16))

    outs = [out(512, BF16), out_t(512), out(512, BF16), out(512, BF16), out(1024, BF16), out(LANES, F32),
            out(512, BF16), out_t(LANES), out(LANES, BF16), out(512, BF16)]
    in_specs = _row_specs(rows, d, tm, HALO_A) + [
        _mod_spec(layer, d, mod_row), _layer_spec(layer, (1, d)), _layer_spec(layer, (d, C_GATE)),
        _layer_spec(layer, (1, LANES)), _layer_spec(layer, (1, LANES)),
        _layer_spec(layer, (SSD_CONV, 1024)), _layer_spec(layer, (1, 1024)),
        _layer_spec(layer, (CONF_CONV, BRANCH_W)), _layer_spec(layer, (1, BRANCH_W)),
        _layer_spec(layer, (1, BRANCH_W)), _layer_spec(layer, (1, BRANCH_W))]
    args = [xs, xs, xs, mod, gpre, w_a, qn, kn, scw, scb, ccw, ccb, lng, lnb]
    if rope is not None:
        in_specs += [pl.BlockSpec((tm, LANES), lambda t: (t % tps, 0))] * 2
        args += list(rope)
    return pl.pallas_call(
        functools.partial(_inproj_kernel, tm, tps, rope is not None),
        grid=(rows // tm,),
        in_specs=in_specs,
        out_specs=[o[0] for o in outs],
        out_shape=[o[1] for o in outs],
        scratch_shapes=[pltpu.VMEM((tm + 2 * HALO_A, 1024), F32),
                        pltpu.VMEM((SUBLANES, tm + 2 * HALO_A, BRANCH_W), F32)],
        compiler_params=_params(1),
        name="inproj",
    )(*args)


def _key_chunks(seg_lens):
    return [(s, off, min(KEY_CHUNK, n - off)) for s, n in enumerate(seg_lens) for off in range(0, n, KEY_CHUNK)]


def _online_softmax_pv(score_fn, value_fn, chunks):
    m, acc = None, None
    ahead = [score_fn(c) for c in chunks[:2]]
    for i, c in enumerate(chunks):
        s = ahead.pop(0)
        if i + 2 < len(chunks):
            ahead.append(score_fn(chunks[i + 2]))
        mc = jnp.max(s, axis=-1, keepdims=True)
        if m is None:
            m = mc
            acc = _dot(jnp.exp2(s - m).astype(BF16), value_fn(c))
        else:
            m_new = jnp.maximum(m, mc)
            acc = acc * jnp.exp2(m - m_new) + _dot(jnp.exp2(s - m_new).astype(BF16), value_fn(c))
            m = m_new
    return acc


def _da_kernel(lambda_init, sub, seg_lens, q_ref, *refs):
    n_seg = len(seg_lens)
    kt_refs, v_refs = refs[0:2 * n_seg:2], refs[1:2 * n_seg:2]
    lam_ref, sub_ref, o_ref = refs[2 * n_seg:]
    chunks = _key_chunks(seg_lens)
    lp = lam_ref[...]
    lam = (jnp.exp(jnp.sum(lp[0:1] * lp[1:2], axis=-1, keepdims=True))
           - jnp.exp(jnp.sum(lp[2:3] * lp[3:4], axis=-1, keepdims=True)) + lambda_init)
    lane = lax.broadcasted_iota(jnp.int32, (sub, LANES), 1)

    def tile(i, carry):
        rs = pl.ds(i * sub, sub)
        for h in range(DA_HEADS):
            blk, half = h // 2, h % 2
            keep = (lane // 64) == half
            q1 = q_ref[rs, blk * LANES:(blk + 1) * LANES]
            q2 = q_ref[rs, (2 + blk) * LANES:(3 + blk) * LANES]
            q1 = jnp.where(keep, q1, jnp.zeros_like(q1))
            q2 = jnp.where(keep, q2, jnp.zeros_like(q2))

            def scores(c):
                s, off, n = c
                s1 = _dot(q1, kt_refs[s][blk * LANES:(blk + 1) * LANES, off:off + n])
                s2 = _dot(q2, kt_refs[s][(2 + blk) * LANES:(3 + blk) * LANES, off:off + n])
                return jnp.concatenate([s1, s2], axis=0)

            def values(c):
                s, off, n = c
                v = v_refs[s][off:off + n, h * LANES:(h + 1) * LANES]
                return jnp.concatenate([v, jnp.ones_like(v)], axis=1)

            acc = _online_softmax_pv(scores, values, chunks)
            o1 = acc[0:sub, 0:LANES] / acc[0:sub, LANES:2 * LANES]
            o2 = acc[sub:2 * sub, 0:LANES] / acc[sub:2 * sub, LANES:2 * LANES]
            o = _rms(o1 - lam * o2, sub_ref[...]) * (1.0 - lambda_init)
            o_ref[rs, h * LANES:(h + 1) * LANES] = o.astype(BF16)
        return carry

    for i in range(q_ref.shape[0] // sub):
        tile(i, 0)


def _diff_attention(q, tq, segs, da_lambda, subln, layer, lambda_init):
    rows = q.shape[0]
    n_batch = segs[0][0].shape[0]
    nq = rows // n_batch // tq
    seg_lens = tuple(kt.shape[2] for kt, _ in segs)
    in_specs = [pl.BlockSpec((tq, 512), lambda b, i: (b * nq + i, 0))]
    args = [q]
    for kt, v in segs:
        n = kt.shape[2]
        in_specs += [pl.BlockSpec((None, 512, n), lambda b, i: (b, 0, 0)),
                     pl.BlockSpec((None, n, 512), lambda b, i: (b, 0, 0))]
        args += [kt, v.reshape(n_batch, n, 512)]
    in_specs += [pl.BlockSpec((None, 4, DA_HEAD_DIM), lambda b, i: (layer, 0, 0)),
                 pl.BlockSpec((None, 1, LANES), lambda b, i: (layer, 0, 0))]
    return pl.pallas_call(
        functools.partial(_da_kernel, lambda_init, min(DA_SUB, tq), seg_lens),
        grid=(n_batch, nq),
        in_specs=in_specs,
        out_specs=pl.BlockSpec((tq, 512), lambda b, i: (b * nq + i, 0)),
        out_shape=jax.ShapeDtypeStruct((rows, 512), BF16),
        compiler_params=_params(2),
        name="diff_attention",
    )(*args, da_lambda, subln)


def _gqa_kernel(sub, seg_lens, q_ref, *refs):
    n_seg = len(seg_lens)
    kt_refs, v_refs = refs[0:2 * n_seg:2], refs[1:2 * n_seg:2]
    o_ref = refs[2 * n_seg]
    chunks = _key_chunks(seg_lens)
    lane = lax.broadcasted_iota(jnp.int32, (sub, LANES), 1)
    lane_st = lax.broadcasted_iota(jnp.int32, (GQA_HPG * sub, LANES), 1)

    def tile(i, carry):
        rs = pl.ds(i * sub, sub)
        for g in range(GQA_KV_HEADS):
            in_group = (lane // 64) == g
            stack = []
            for r in range(GQA_HPG):
                head = g * GQA_HPG + r
                qb = q_ref[rs, (head // 2) * LANES:(head // 2 + 1) * LANES].astype(F32)
                if head % 2 != g:
                    qb = pltpu.roll(qb, 64, axis=1)
                stack.append(jnp.where(in_group, qb, 0.0).astype(BF16))
            qst = jnp.concatenate(stack, axis=0)

            def scores(c):
                s, off, n = c
                return _dot(qst, kt_refs[s][:, off:off + n])

            def values(c):
                s, off, n = c
                v = v_refs[s][off:off + n, :]
                vl = lax.broadcasted_iota(jnp.int32, v.shape, 1)
                return jnp.where((vl // 64) == g, v, jnp.ones_like(v))

            acc = _online_softmax_pv(scores, values, chunks)
            den = jnp.where((lane_st // 64) == g, pltpu.roll(acc, 64, axis=1), 1.0)
            o = acc / den
            for pair in range(GQA_HPG // 2):
                pieces = []
                for half in range(2):
                    r = 2 * pair + half
                    piece = o[r * sub:(r + 1) * sub]
                    pieces.append(piece if half == g else pltpu.roll(piece, 64, axis=1))
                blk = (g * GQA_HPG) // 2 + pair
                o_ref[rs, blk * LANES:(blk + 1) * LANES] = jnp.where(lane < 64, pieces[0], pieces[1]).astype(BF16)
        return carry

    for i in range(q_ref.shape[0] // sub):
        tile(i, 0)


def _gqa_attention(q, tq, segs):
    rows = q.shape[0]
    n_batch = segs[0][0].shape[0]
    nq = rows // n_batch // tq
    seg_lens = tuple(kt.shape[2] for kt, _ in segs)
    in_specs = [pl.BlockSpec((tq, 512), lambda b, i: (b * nq + i, 0))]
    args = [q]
    for kt, v in segs:
        n = kt.shape[2]
        in_specs += [pl.BlockSpec((None, LANES, n), lambda b, i: (b, 0, 0)),
                     pl.BlockSpec((None, n, LANES), lambda b, i: (b, 0, 0))]
        args += [kt, v.reshape(n_batch, n, LANES)]
    return pl.pallas_call(
        functools.partial(_gqa_kernel, min(GQA_SUB, tq), seg_lens),
        grid=(n_batch, nq),
        in_specs=in_specs,
        out_specs=pl.BlockSpec((tq, 512), lambda b, i: (b * nq + i, 0)),
        out_shape=jax.ShapeDtypeStruct((rows, 512), BF16),
        compiler_params=_params(2),
        name="gqa_attention",
    )(*args)


def _ssd_direction(d, x_ref, dt_ref, alog_ref, bias_ref, skip_ref, y_ref, h_ref):
    q = CHUNK
    lane = lax.broadcasted_iota(jnp.int32, (1, LANES), 1)
    a_vec = jnp.where(lane < 2 * SSD_HEADS, -jnp.exp(alog_ref[...]), 0.0)
    raw = dt_ref[...] + bias_ref[...]
    dt = jnp.maximum(raw, 0.0) + jnp.log1p(jnp.exp(-jnp.abs(raw)))
    da = dt * a_vec
    ri = lax.broadcasted_iota(jnp.int32, (q, q), 0)
    ci = lax.broadcasted_iota(jnp.int32, (q, q), 1)
    tri = (ci <= ri) if d == 0 else (ci >= ri)
    hi = da.astype(BF16).astype(F32)
    rest = da - hi
    mid = rest.astype(BF16).astype(F32)
    packed = hi + pltpu.roll(mid, 2 * SSD_HEADS, axis=1) + pltpu.roll(rest - mid, 4 * SSD_HEADS, axis=1)
    part = _dot(tri.astype(BF16), packed.astype(BF16))
    cs = part + pltpu.roll(part, LANES - 2 * SSD_HEADS, axis=1) + pltpu.roll(part, LANES - 4 * SSD_HEADS, axis=1)
    cs = jnp.where(lane < 2 * SSD_HEADS, cs, 0.0)
    cs_t = cs.T
    tot = jnp.sum(da, axis=0, keepdims=True)

    er = lax.broadcasted_iota(jnp.int32, (2 * LANES, 512), 0) % LANES
    ec = lax.broadcasted_iota(jnp.int32, (2 * LANES, 512), 1) // SSD_HEAD_DIM
    expand = (er == d * SSD_HEADS + ec).astype(BF16)
    stack = jnp.concatenate([dt, jnp.exp(tot - cs), jnp.exp(cs), jnp.broadcast_to(jnp.exp(tot), (8, LANES))], axis=0)
    hi, lo = _split2(stack)
    wide = _dot(jnp.concatenate([hi, lo], axis=1), expand)
    dt_e, dte_e, ecs_e, cdec_e = wide[0:q], wide[q:2 * q], wide[2 * q:3 * q], wide[3 * q:3 * q + 1]

    xs = x_ref[:, 0:512].astype(F32)
    xsdt = xs * dt_e
    lane_g = lax.broadcasted_iota(jnp.int32, (q, 2 * LANES), 1) // SSD_HEAD_DIM
    for g in range(SSD_GROUPS):
        gl = slice(g * 2 * LANES, (g + 1) * 2 * LANES)
        bm = x_ref[:, 512 + g * LANES:512 + (g + 1) * LANES].astype(F32)
        cm = x_ref[:, 768 + g * LANES:768 + (g + 1) * LANES]
        cb = _dot_nt(cm, bm.astype(BF16))
        xg = xsdt[:, gl]
        y = _dot(cm, h_ref[g].astype(BF16)) * ecs_e[:, gl]
        for r in range(SSD_HPG):
            idx = d * SSD_HEADS + g * SSD_HPG + r
            seg = cs[:, idx:idx + 1] - cs_t[idx:idx + 1, :]
            lmat = jnp.exp(jnp.where(tri, seg, -jnp.inf))
            y = y + _dot((cb * lmat).astype(BF16), jnp.where(lane_g == r, xg, 0.0).astype(BF16))
        y_ref[:, gl] = y + skip_ref[:, gl] * xs[:, gl]
        h_ref[g] = cdec_e[:, gl] * h_ref[g] + _dot(bm.T.astype(BF16), (xg * dte_e[:, gl]).astype(BF16))


def _ssd_kernel(xf, dtf, xb, dtb, xc, dtc, alog, bias, skip, yf, yb, yfc, ybc, h_ref):
    t = pl.program_id(1)

    @pl.when(t == 0)
    def _():
        h_ref[...] = jnp.zeros_like(h_ref)
        yf[...] = jnp.zeros_like(yf)
        yb[...] = jnp.zeros_like(yb)
        _ssd_direction(0, xc, dtc, alog, bias, skip.at[0], yfc, h_ref.at[0])
        _ssd_direction(1, xc, dtc, alog, bias, skip.at[1], ybc, h_ref.at[1])

    @pl.when(t != 0)
    def _():
        _ssd_direction(0, xf, dtf, alog, bias, skip.at[0], yf, h_ref.at[0])
        _ssd_direction(1, xb, dtb, alog, bias, skip.at[1], yb, h_ref.at[1])


def _ssd(xc_l, dt_l, xc_c, dt_c, alog, bias, skip, layer, n_batch):
    rows_l, rows_c = xc_l.shape[0], xc_c.shape[0]
    n_l = rows_l // n_batch // CHUNK

    def fwd(b, t):
        return (b * n_l + jnp.maximum(t - 1, 0), 0)

    def bwd(b, t):
        return (b * n_l + n_l - jnp.maximum(t, 1), 0)

    def cxt(b, t):
        return (b, 0)

    def lyr(shape):
        return pl.BlockSpec((None,) + shape, lambda b, t: (layer,) + (0,) * len(shape))

    return pl.pallas_call(
        _ssd_kernel,
        grid=(n_batch, n_l + 1),
        in_specs=[pl.BlockSpec((CHUNK, 1024), fwd), pl.BlockSpec((CHUNK, LANES), fwd),
                  pl.BlockSpec((CHUNK, 1024), bwd), pl.BlockSpec((CHUNK, LANES), bwd),
                  pl.BlockSpec((CHUNK, 1024), cxt), pl.BlockSpec((CHUNK, LANES), cxt),
                  lyr((1, LANES)), lyr((1, LANES)), lyr((2, 1, 512))],
        out_specs=[pl.BlockSpec((CHUNK, 512), fwd), pl.BlockSpec((CHUNK, 512), bwd),
                   pl.BlockSpec((CHUNK, 512), cxt), pl.BlockSpec((CHUNK, 512), cxt)],
        out_shape=[jax.ShapeDtypeStruct((rows_l, 512), F32)] * 2 + [jax.ShapeDtypeStruct((rows_c, 512), F32)] * 2,
        scratch_shapes=[pltpu.VMEM((2, SSD_GROUPS, SSD_STATE, 2 * LANES), F32)],
        compiler_params=_params(2),
        name="ssd_scan",
    )(xc_l, dt_l, xc_l, dt_l, xc_c, dt_c, alog, bias, skip)


def _merge_kernel(x_ref, mod, gpre, gpost, oa, yf, yb, z, ssdn, og, cf, wg0, wg1, wg2, wg3, wb, wo, o_ref):
    x = x_ref[...]
    h = (_rms(x, gpre[...]) * (1.0 + mod[1:2, :]) + mod[0:1, :]).astype(BF16)
    ys = _rms((yf[...] + yb[...]) * _silu(z[...].astype(F32)), ssdn[...]).astype(BF16)
    branches = [oa[...], ys, og[...], cf[...]]
    gates = [wg0, wg1, wg2, wg3]
    mixed = None
    for k in range(N_BRANCH):
        term = _sigmoid(_dot(h, gates[k][...])) * _dot(branches[k], wb[k])
        mixed = term if mixed is None else mixed + term
    out = _dot(mixed.astype(BF16), wo[...])
    o_ref[...] = x + mod[2:3, :] * _rms(out, gpost[...])


def _merge(xs, tm, layer, mod, mod_row, gpre, gpost, oa, yf, yb, z, ssdn, og, cf, w_all, w_branch, w_out):
    rows, d = xs.shape

    def tile(width):
        return pl.BlockSpec((tm, width), lambda t: (t, 0))

    return pl.pallas_call(
        _merge_kernel,
        grid=(rows // tm,),
        in_specs=[tile(d), _mod_spec(layer, d, mod_row), _layer_spec(layer, (1, d)), _layer_spec(layer, (1, d)),
                  tile(512), tile(512), tile(512), tile(512), _layer_spec(layer, (1, BRANCH_W)), tile(512), tile(512),
                  *[_resident((None, d, d), lambda t, k=k: (layer, 0, C_GATE // d + k)) for k in range(N_BRANCH)],
                  _layer_spec(layer, (N_BRANCH, BRANCH_W, d)),
                  _layer_spec(layer, (d, d))],
        out_specs=tile(d),
        out_shape=jax.ShapeDtypeStruct((rows, d), F32),
        compiler_params=_params(1),
        name="merge",
    )(xs, mod, gpre, gpost, oa, yf, yb, z, ssdn, og, cf, w_all, w_all, w_all, w_all, w_branch, w_out)


FFN_COL_CHUNK = 512


def _ffn_chunks(d_ff):
    return [(c0, min(FFN_COL_CHUNK, d_ff - c0)) for c0 in range(0, d_ff, FFN_COL_CHUNK)]


def _ffn_kernel(tm, tiles_per_seg, d_ff, xm, xp, xn, mod, gpre, gpost, wu, cw_ref, cb_ref, wd, o_ref, abuf, gbuf):
    ti = pl.program_id(0) % tiles_per_seg
    prev_ok = ti > 0
    next_ok = ti < tiles_per_seg - 1
    shift, scale, gain = mod[3:4, :], mod[4:5, :], gpre[...]

    def normmod(v):
        return _rms(v, gain) * (1.0 + scale) + shift

    x = xm[...]
    hext = jnp.concatenate([jnp.where(prev_ok, normmod(xp[...]), 0.0), normmod(x),
                            jnp.where(next_ok, normmod(xn[...]), 0.0)], axis=0).astype(BF16)

    def conv(buf, c0, cw):
        acc = cb_ref[:, c0:c0 + cw] + cw_ref[0:1, c0:c0 + cw] * buf[pl.ds(HALO_F - 1, tm), 0:cw]
        for k in range(1, FFN_CONV):
            acc = acc + cw_ref[k:k + 1, c0:c0 + cw] * buf[pl.ds(HALO_F - 1 + k, tm), 0:cw]
        return acc

    chunks = _ffn_chunks(d_ff)

    def up(i):
        c0, cw = chunks[i]
        abuf[i % 3, :, 0:cw] = _dot(hext, wu[:, c0:c0 + cw])
        gbuf[i % 3, :, 0:cw] = _dot(hext, wu[:, d_ff + c0:d_ff + c0 + cw])

    up(0)
    up(1)
    out = None
    for i, (c0, cw) in enumerate(chunks):
        if i + 2 < len(chunks):
            up(i + 2)
        act = (_silu(conv(gbuf.at[i % 3], d_ff + c0, cw)) * conv(abuf.at[i % 3], c0, cw)).astype(BF16)
        part = _dot(act, wd[c0:c0 + cw, :])
        out = part if out is None else out + part
    o_ref[...] = x + mod[5:6, :] * _rms(out, gpost[...])


def _ffn(xs, seg_len, tm, layer, mod, mod_row, gpre, gpost, w_up, conv_w, conv_b, w_down):
    rows, d = xs.shape
    d_ff = w_down.shape[1]
    return pl.pallas_call(
        functools.partial(_ffn_kernel, tm, seg_len // tm, d_ff),
        grid=(rows // tm,),
        in_specs=_row_specs(rows, d, tm, HALO_F) + [
            _mod_spec(layer, d, mod_row), _layer_spec(layer, (1, d)), _layer_spec(layer, (1, d)),
            _layer_spec(layer, (d, 2 * d_ff)), _layer_spec(layer, (FFN_CONV, 2 * d_ff)),
            _layer_spec(layer, (1, 2 * d_ff)), _layer_spec(layer, (d_ff, d))],
        out_specs=pl.BlockSpec((tm, d), lambda t: (t, 0)),
        out_shape=jax.ShapeDtypeStruct((rows, d), F32),
        scratch_shapes=[pltpu.VMEM((3, tm + 2 * HALO_F, FFN_COL_CHUNK), F32)] * 2,
        compiler_params=_params(1),
        name="conv_ffn",
    )(xs, xs, xs, mod, gpre, gpost, w_up, conv_w, conv_b, w_down)


def _rope_tables(seq):
    rows = seq // GRID_W
    row = jnp.repeat(jnp.arange(rows, dtype=F32), GRID_W)
    col = jnp.tile(jnp.arange(GRID_W, dtype=F32), rows)
    quarter = DA_HEAD_DIM // 4
    inv_freq = ROPE_BASE ** (-jnp.arange(quarter, dtype=F32) / quarter)
    ang = jnp.concatenate([row[:, None] * inv_freq, col[:, None] * inv_freq], axis=-1)
    cos, sin = jnp.cos(ang), jnp.sin(ang)
    return jnp.concatenate([cos] * 4, axis=-1), jnp.concatenate([-sin, sin, -sin, sin], axis=-1)


def kernel(x, c, ctx, c_ctx, w_mod, b_mod, mix_norm_pre, mix_norm_post, w_in, da_lambda, da_subln, ssd_conv_w, ssd_conv_b, ssd_a_log, ssd_dt_bias, ssd_d, ssd_norm, gqa_q_norm, gqa_k_norm, conf_conv_w, conf_conv_b, conf_ln_g, conf_ln_b, w_branch, w_out, ffn_norm_pre, ffn_norm_post, ffn_w_up, ffn_conv_w, ffn_conv_b, ffn_w_down):
    n_batch, seq, d = x.shape
    n_ctx = ctx.shape[1]
    depth = w_mod.shape[0]
    tm_l = min(TM_LATENT, seq)
    tq = min(TQ, seq)
    assert n_ctx == CHUNK and seq % tm_l == 0 and seq % tq == 0 and seq % GRID_W == 0 and n_batch < 8

    o_gq = C_DT + 2 * SSD_HEADS
    o_gate = w_in.shape[-1] - N_BRANCH * d
    assert d == 1024 and C_GATE % d == 0 and o_gate - o_gq == C_CONF + 2 * BRANCH_W - C_GQ
    w_a = jnp.concatenate([w_in[:, :, :o_gq], jnp.zeros((depth, d, C_GQ - o_gq), F32),
                           w_in[:, :, o_gq:o_gate], jnp.zeros((depth, d, C_GATE - C_CONF - 2 * BRANCH_W), F32),
                           w_in[:, :, o_gate:]], axis=-1).astype(BF16)
    w_br = w_branch.astype(BF16)
    w_o = w_out.astype(BF16)
    w_up = ffn_w_up.astype(BF16)
    w_dn = ffn_w_down.astype(BF16)

    def row3(p):
        return p.reshape(depth, 1, -1)

    def pad_lanes(p):
        flat = p.reshape(depth, 1, -1)
        return jnp.pad(flat, ((0, 0), (0, 0), (0, LANES - flat.shape[-1])))

    qn = row3(jnp.concatenate([gqa_q_norm, gqa_q_norm], axis=-1))
    kn = row3(jnp.concatenate([gqa_k_norm, gqa_k_norm], axis=-1))
    skip = jnp.repeat(ssd_d, SSD_HEAD_DIM, axis=-1).reshape(depth, 2, 1, SSD_HEADS * SSD_HEAD_DIM)
    rope = _rope_tables(seq)

    cvec = jnp.concatenate([c, c_ctx[None, :], jnp.zeros((8 - n_batch - 1, d), F32)], axis=0)
    mod = _modulation(cvec, w_mod, b_mod).reshape(depth, 8, 6, d)
    tiles_l = seq // tm_l

    def row_latent(t):
        return t // tiles_l

    def row_ctx(t):
        return n_batch

    xl = x.reshape(n_batch * seq, d)
    xc = ctx.reshape(n_batch * n_ctx, d)
    for l in range(depth):
        need_ctx = l < depth - 1
        lambda_init = 0.8 - 0.6 * math.exp(-0.3 * l)
        proj = (row3(mix_norm_pre), w_a, qn, kn, ssd_conv_w, row3(ssd_conv_b), conf_conv_w, row3(conf_conv_b),
                row3(conf_ln_g), row3(conf_ln_b))
        qa, kat, va, z, sx, dt, gq, gkt, gv, cf = _inproj(xl, seq, tm_l, l, mod, row_latent, rope, *proj)
        qa_c, kat_c, va_c, z_c, sx_c, dt_c, gq_c, gkt_c, gv_c, cf_c = _inproj(
            xc, n_ctx, n_ctx, l, mod, row_ctx, None, *proj)
        oa = _diff_attention(qa, tq, [(kat, va), (kat_c, va_c)], da_lambda, row3(da_subln), l, lambda_init)
        og = _gqa_attention(gq, tq, [(gkt, gv), (gkt_c, gv_c)])
        yf, yb, yf_c, yb_c = _ssd(sx, dt, sx_c, dt_c, pad_lanes(ssd_a_log), pad_lanes(ssd_dt_bias), skip, l, n_batch)
        mix = (row3(mix_norm_pre), row3(mix_norm_post))
        mixw = (w_a, w_br, w_o)
        ffn = (row3(ffn_norm_pre), row3(ffn_norm_post), w_up, ffn_conv_w, row3(ffn_conv_b), w_dn)
        xl = _merge(xl, tm_l, l, mod, row_latent, *mix, oa, yf, yb, z, row3(ssd_norm), og, cf, *mixw)
        xl = _ffn(xl, seq, tm_l, l, mod, row_latent, *ffn)
        if need_ctx:
            oa_c = _diff_attention(qa_c, n_ctx, [(kat_c, va_c)], da_lambda, row3(da_subln), l, lambda_init)
            og_c = _gqa_attention(gq_c, n_ctx, [(gkt_c, gv_c)])
            xc = _merge(xc, min(tm_l, n_batch * n_ctx), l, mod, row_ctx, *mix, oa_c, yf_c, yb_c, z_c, row3(ssd_norm), og_c, cf_c, *mixw)
            xc = _ffn(xc, n_ctx, n_ctx, l, mod, row_ctx, *ffn)
    return xl.reshape(n_batch, seq, d)
```
